```python
import math, functools
import jax, jax.numpy as jnp
from jax import lax
import numpy as np

D_MODEL = 1024
BATCH = 4
SEQ = 8192
DEPTH = 4

CONV_CH = 512
N_HEADS = 8
HEAD_DIM = 64
ATTN_CH = N_HEADS * HEAD_DIM
D_MIX = CONV_CH + ATTN_CH
D_IN = 2 * CONV_CH + 3 * ATTN_CH
CONV_K = 31
GRID_W = 64
WIN_R = 8
WIN_C = 16
Q_CB = 16
N_CB = GRID_W // Q_CB
BAND_C = Q_CB + WIN_C
D_FF = 2816
N_EXPERTS = 8
TOP_K = 2
D_FF_EXPERT = 3584
MOE_BLK = 128
N_DENSE = (DEPTH + 1) // 2
N_MOE = DEPTH // 2
D_PLE = 256
LN_EPS = 1e-5
DEEPNORM_ALPHA = (2.0 * DEPTH) ** 0.25
DEEPNORM_BETA = (8.0 * DEPTH) ** -0.25

kernel_name = "hybrid_conformer_natten_moe_deepnorm_encoder"


def layer_norm(x, g, b):
    xf = x.astype(jnp.float32)
    mu = jnp.mean(xf, axis=-1, keepdims=True)
    xc = xf - mu
    var = jnp.mean(xc * xc, axis=-1, keepdims=True)
    y = xc * lax.rsqrt(var + LN_EPS)
    return (y * g.astype(jnp.float32) + b.astype(jnp.float32)).astype(x.dtype)


def conformer_conv(a, g, w, b, ln_g, ln_b):
    u = a * jax.nn.sigmoid(g)
    c = u.shape[-1]
    pad = CONV_K // 2
    y = lax.conv_general_dilated(
        u, w[:, None, :], window_strides=(1,), padding=[(pad, pad)],
        dimension_numbers=("NWC", "WIO", "NWC"), feature_group_count=c)
    y = layer_norm(y + b, ln_g, ln_b)
    return jax.nn.silu(y)


def neighbourhood_attention(q, k, v, rpb):
    B, T, H, Dh = q.shape
    rows = T // GRID_W
    kr = min(WIN_R, rows)
    scale = Dh ** -0.5
    band0 = np.clip(np.arange(N_CB) * Q_CB - WIN_C // 2, 0, GRID_W - BAND_C)
    key_col = band0[:, None] + np.arange(BAND_C)
    q_col = np.arange(GRID_W).reshape(N_CB, Q_CB)
    win0 = np.clip(q_col - WIN_C // 2, 0, GRID_W - WIN_C)
    col_mask = ((key_col[:, None, :] >= win0[:, :, None]) &
                (key_col[:, None, :] < win0[:, :, None] + WIN_C))
    dc_idx = np.clip(key_col[:, None, :] - q_col[:, :, None] + WIN_C - 1, 0, 2 * WIN_C - 2)
    rpb_c = rpb[:, :, dc_idx]
    kg = k.reshape(B, rows, GRID_W, H, Dh)
    vg = v.reshape(B, rows, GRID_W, H, Dh)
    qr = jnp.moveaxis(q.reshape(B, rows, N_CB, Q_CB, H, Dh), 1, 0)

    def row_fn(args):
        r, q_row = args
        r0 = jnp.clip(r - kr // 2, 0, rows - kr)
        k_band = lax.dynamic_slice_in_dim(kg, r0, kr, axis=1)[:, :, key_col]
        v_band = lax.dynamic_slice_in_dim(vg, r0, kr, axis=1)[:, :, key_col]
        s = jnp.einsum("bjqhd,bijkhd->bhjqik", q_row, k_band).astype(jnp.float32) * scale
        dr_idx = r0 + jnp.arange(kr) - r + WIN_R - 1
        bias = jnp.transpose(jnp.take(rpb_c, dr_idx, axis=1), (0, 2, 3, 1, 4))
        s = jnp.where(col_mask[:, :, None, :], s + bias.astype(jnp.float32), -jnp.inf)
        w = jax.nn.softmax(s.reshape(B, H, N_CB, Q_CB, kr * BAND_C), axis=-1)
        w = w.reshape(s.shape).astype(v.dtype)
        return jnp.einsum("bhjqik,bijkhd->bjqhd", w, v_band)

    out = lax.map(row_fn, (jnp.arange(rows), qr))
    return jnp.moveaxis(out, 0, 1).reshape(B, T, H * Dh)


def swiglu(x, w1, w3, w2):
    return (jax.nn.silu(x @ w1) * (x @ w3)) @ w2


def moe_swiglu(x, w_router, w1, w3, w2):
    B, T, D = x.shape
    xf = x.reshape(-1, D)
    n = xf.shape[0]
    logits = (xf @ w_router).astype(jnp.float32)
    top_vals, top_idx = lax.top_k(logits, TOP_K)
    gates = jax.nn.softmax(top_vals, axis=-1)
    n_assign = n * TOP_K
    e_flat = top_idx.reshape(-1)
    tok_flat = jnp.repeat(jnp.arange(n, dtype=jnp.int32), TOP_K)
    g_flat = gates.reshape(-1)
    order = jnp.argsort(e_flat)
    e_sorted = e_flat[order]
    tok_sorted = tok_flat[order]
    g_sorted = g_flat[order]
    counts = jnp.bincount(e_flat, length=N_EXPERTS)
    starts = jnp.cumsum(counts) - counts
    padded = (counts + MOE_BLK - 1) // MOE_BLK * MOE_BLK
    pends = jnp.cumsum(padded)
    pstarts = pends - padded
    dest = pstarts[e_sorted] + (jnp.arange(n_assign) - starts[e_sorted])
    n_blocks = -(-n_assign // MOE_BLK) + N_EXPERTS
    n_slots = n_blocks * MOE_BLK
    tok_buf = jnp.zeros((n_slots,), jnp.int32).at[dest].set(tok_sorted)
    gate_buf = jnp.zeros((n_slots,), jnp.float32).at[dest].set(g_sorted)
    block_e = jnp.minimum(
        jnp.searchsorted(pends, jnp.arange(n_blocks) * MOE_BLK, side="right"), N_EXPERTS - 1)

    def block_fn(args):
        tok, e = args
        xb = xf[tok]
        return (jax.nn.silu(xb @ w1[e]) * (xb @ w3[e])) @ w2[e]

    y = lax.map(block_fn, (tok_buf.reshape(n_blocks, MOE_BLK), block_e)).reshape(n_slots, D)
    y = y * gate_buf[:, None].astype(y.dtype)
    out = jax.ops.segment_sum(y, tok_buf, num_segments=n)
    return out.reshape(B, T, D)


def setup_inputs(seed: int = 0) -> dict:
    key = jax.random.key(seed)
    ks = jax.random.split(key, 32)
    f32 = jnp.float32
    nrm = lambda k, shape, s: jax.random.normal(k, shape, f32) * s
    gain = lambda k, shape: 1.0 + 0.02 * jax.random.normal(k, shape, f32)
    return {
        "x": jax.random.normal(ks[0], (BATCH, SEQ, D_MODEL), f32),
        "p": jax.random.normal(ks[1], (DEPTH, BATCH, SEQ, D_PLE), f32),
        "ln_in_g": gain(ks[2], (D_MODEL,)),
        "ln_in_b": nrm(ks[3], (D_MODEL,), 0.02),
        "w_in": nrm(ks[4], (DEPTH, D_MODEL, D_IN), D_MODEL ** -0.5),
        "b_in": nrm(ks[5], (DEPTH, D_IN), 0.02),
        "conv_w": nrm(ks[6], (DEPTH, CONV_K, CONV_CH), CONV_K ** -0.5),
        "conv_b": nrm(ks[7], (DEPTH, CONV_CH), 0.02),
        "conv_ln_g": gain(ks[8], (DEPTH, CONV_CH)),
        "conv_ln_b": nrm(ks[9], (DEPTH, CONV_CH), 0.02),
        "rpb": nrm(ks[10], (DEPTH, N_HEADS, 2 * WIN_R - 1, 2 * WIN_C - 1), 0.02),
        "w_out": nrm(ks[11], (DEPTH, D_MIX, D_MODEL), D_MIX ** -0.5 * DEEPNORM_BETA),
        "b_out": nrm(ks[12], (DEPTH, D_MODEL), 0.02),
        "ln1_g": gain(ks[13], (DEPTH, D_MODEL)),
        "ln1_b": nrm(ks[14], (DEPTH, D_MODEL), 0.02),
        "ffn_w1": nrm(ks[15], (N_DENSE, D_MODEL, D_FF), D_MODEL ** -0.5),
        "ffn_w3": nrm(ks[16], (N_DENSE, D_MODEL, D_FF), D_MODEL ** -0.5),
        "ffn_w2": nrm(ks[17], (N_DENSE, D_FF, D_MODEL), D_FF ** -0.5 * DEEPNORM_BETA),
        "w_router": nrm(ks[18], (N_MOE, D_MODEL, N_EXPERTS), D_MODEL ** -0.5),
        "moe_w1": nrm(ks[19], (N_MOE, N_EXPERTS, D_MODEL, D_FF_EXPERT), D_MODEL ** -0.5),
        "moe_w3": nrm(ks[20], (N_MOE, N_EXPERTS, D_MODEL, D_FF_EXPERT), D_MODEL ** -0.5),
        "moe_w2": nrm(ks[21], (N_MOE, N_EXPERTS, D_FF_EXPERT, D_MODEL), D_FF_EXPERT ** -0.5 * DEEPNORM_BETA),
        "ple_w": nrm(ks[22], (DEPTH, D_PLE, D_MODEL), D_PLE ** -0.5 * DEEPNORM_BETA),
        "ple_gate_w": nrm(ks[23], (DEPTH, D_MODEL, D_MODEL), D_MODEL ** -0.5),
        "ple_gate_b": nrm(ks[24], (DEPTH, D_MODEL), 0.02),
        "ln2_g": gain(ks[25], (DEPTH, D_MODEL)),
        "ln2_b": nrm(ks[26], (DEPTH, D_MODEL), 0.02),
    }


def reference(x, p, ln_in_g, ln_in_b, w_in, b_in, conv_w, conv_b, conv_ln_g, conv_ln_b,
              rpb, w_out, b_out, ln1_g, ln1_b, ffn_w1, ffn_w3, ffn_w2, w_router,
              moe_w1, moe_w3, moe_w2, ple_w, ple_gate_w, ple_gate_b, ln2_g, ln2_b):
    B, T, _ = x.shape
    h = layer_norm(x, ln_in_g, ln_in_b)
    for i in range(DEPTH):
        proj = h @ w_in[i] + b_in[i]
        a = proj[..., :CONV_CH]
        g = proj[..., CONV_CH:2 * CONV_CH]
        qkv = proj[..., 2 * CONV_CH:].reshape(B, T, 3, N_HEADS, HEAD_DIM)
        conv_out = conformer_conv(a, g, conv_w[i], conv_b[i], conv_ln_g[i], conv_ln_b[i])
        attn_out = neighbourhood_attention(qkv[:, :, 0], qkv[:, :, 1], qkv[:, :, 2], rpb[i])
        mix = jnp.concatenate([conv_out, attn_out], axis=-1) @ w_out[i] + b_out[i]
        h = layer_norm(DEEPNORM_ALPHA * h + mix, ln1_g[i], ln1_b[i])
        j = i // 2
        if i % 2 == 0:
            f = swiglu(h, ffn_w1[j], ffn_w3[j], ffn_w2[j])
        else:
            f = moe_swiglu(h, w_router[j], moe_w1[j], moe_w3[j], moe_w2[j])
        e = jax.nn.sigmoid(h @ ple_gate_w[i] + ple_gate_b[i]) * (p[i] @ ple_w[i])
        h = layer_norm(DEEPNORM_ALPHA * h + f + e, ln2_g[i], ln2_b[i])
    return h
```

```python
import functools

import numpy as np
import jax
import jax.numpy as jnp
from jax import lax
from jax.experimental import pallas as pl
from jax.experimental.pallas import tpu as pltpu

F32 = jnp.float32
BF16 = jnp.bfloat16

CONV_CH = 512
N_HEADS = 8
HEAD_DIM = 64
ATTN_CH = N_HEADS * HEAD_DIM
CONV_K = 31
GRID_W = 64
WIN_R = 8
WIN_C = 16
N_EXPERTS = 8
LN_EPS = 1e-5
MASK_NEG = -1e30

LANES = 128
MXU_N = 256
VMEM_LIMIT = 56 * 1024 * 1024

TM_PROJ = 512
TT_CONV = 256
CONV_HALO = 16
CONV_CHUNK = 32
ROWS_PER_ATTN_STEP = 8
TM_MOE = 512
FC_MOE = 1792
TM_ROWDMA = 512


def _cparams(semantics):
    return pltpu.CompilerParams(dimension_semantics=semantics, vmem_limit_bytes=VMEM_LIMIT)


def _layer_norm(x, g, b):
    mu = jnp.mean(x, axis=-1, keepdims=True)
    xc = x - mu
    var = jnp.mean(xc * xc, axis=-1, keepdims=True)
    return xc * lax.rsqrt(var + LN_EPS) * g + b


def _const_spec(shape):
    nd = len(shape)
    return pl.BlockSpec(shape, lambda *_: (0,) * nd, pipeline_mode=pl.Buffered(1))


def _row_spec(tm, width):
    return pl.BlockSpec((tm, width), lambda i: (i, 0))


def _inproj_body(x_ref, lng_ref, lnb_ref, w_ref, b_ref, *out_refs, pre_ln):
    if pre_ln:
        h_ref, u_ref, q_ref, k_ref, v_ref = out_refs
    else:
        u_ref, q_ref, k_ref, v_ref = out_refs
    x = x_ref[...]
    if pre_ln:
        x = _layer_norm(x, lng_ref[...], lnb_ref[...])
        h_ref[...] = x
    xb = x.astype(BF16)

    def proj(lo, hi):
        return jnp.dot(xb, w_ref[:, lo:hi], preferred_element_type=F32) + b_ref[:, lo:hi]

    a = proj(0, CONV_CH)
    g = proj(CONV_CH, 2 * CONV_CH)
    u_ref[...] = a * jax.nn.sigmoid(g)
    base = 2 * CONV_CH
    q_ref[...] = proj(base, base + ATTN_CH).astype(BF16)
    k_ref[...] = proj(base + ATTN_CH, base + 2 * ATTN_CH).astype(BF16)
    v_ref[...] = proj(base + 2 * ATTN_CH, base + 3 * ATTN_CH).astype(BF16)


def _inproj(x, ln_g, ln_b, w, b, pre_ln):
    n, d = x.shape
    d_in = w.shape[1]
    tm = TM_PROJ
    out_shape = [jax.ShapeDtypeStruct((n, CONV_CH), F32)] + [jax.ShapeDtypeStruct((n, ATTN_CH), BF16)] * 3
    out_specs = [_row_spec(tm, CONV_CH)] + [_row_spec(tm, ATTN_CH)] * 3
    if pre_ln:
        out_shape = [jax.ShapeDtypeStruct((n, d), F32)] + out_shape
        out_specs = [_row_spec(tm, d)] + out_specs
    return pl.pallas_call(
        functools.partial(_inproj_body, pre_ln=pre_ln),
        grid=(n // tm,),
        in_specs=[_row_spec(tm, d), _const_spec((1, d)), _const_spec((1, d)),
                  _const_spec((d, d_in)), _const_spec((1, d_in))],
        out_specs=out_specs,
        out_shape=out_shape,
        compiler_params=_cparams(("parallel",)),
        name="inproj_ln" if pre_ln else "inproj",
    )(x, ln_g, ln_b, w, b)


def _conv_body(prev_ref, cur_ref, next_ref, w_ref, b_ref, lng_ref, lnb_ref, o_ref, ext_ref, *, tiles_per_seq):
    i = pl.program_id(0)
    tt = cur_ref.shape[0]
    pos = i % tiles_per_seq
    zeros = jnp.zeros((CONV_HALO, CONV_CH), F32)
    ext_ref[0:CONV_HALO, :] = jnp.where(pos > 0, prev_ref[...], zeros)
    ext_ref[CONV_HALO:CONV_HALO + tt, :] = cur_ref[...]
    ext_ref[CONV_HALO + tt:, :] = jnp.where(pos < tiles_per_seq - 1, next_ref[...], zeros)
    shift = CONV_HALO - CONV_K // 2
    for c in range(tt // CONV_CHUNK):
        base = c * CONV_CHUNK
        acc = jnp.zeros((CONV_CHUNK, CONV_CH), F32)
        for k in range(CONV_K):
            lo = base + k + shift
            acc = acc + ext_ref[lo:lo + CONV_CHUNK, :] * w_ref[k:k + 1, :]
        y = _layer_norm(acc + b_ref[...], lng_ref[...], lnb_ref[...])
        o_ref[base:base + CONV_CHUNK, :] = (y * jax.nn.sigmoid(y)).astype(BF16)


def _conv_module(u, w, b, ln_g, ln_b, seq_len):
    n = u.shape[0]
    tt = TT_CONV
    hb = tt // CONV_HALO
    n_halo_blocks = n // CONV_HALO
    return pl.pallas_call(
        functools.partial(_conv_body, tiles_per_seq=seq_len // tt),
        grid=(n // tt,),
        in_specs=[
            pl.BlockSpec((CONV_HALO, CONV_CH), lambda i: (jnp.maximum(i * hb - 1, 0), 0)),
            pl.BlockSpec((tt, CONV_CH), lambda i: (i, 0)),
            pl.BlockSpec((CONV_HALO, CONV_CH), lambda i: (jnp.minimum((i + 1) * hb, n_halo_blocks - 1), 0)),
            _const_spec((CONV_K, CONV_CH)), _const_spec((1, CONV_CH)),
            _const_spec((1, CONV_CH)), _const_spec((1, CONV_CH)),
        ],
        out_specs=pl.BlockSpec((tt, CONV_CH), lambda i: (i, 0)),
        out_shape=jax.ShapeDtypeStruct((n, CONV_CH), BF16),
        scratch_shapes=[pltpu.VMEM((tt + 2 * CONV_HALO, CONV_CH), F32)],
        compiler_params=_cparams(("parallel",)),
        name="conv_module",
    )(u, u, u, w, b, ln_g, ln_b)


def _attn_bias_table(rpb):
    c = np.arange(GRID_W)
    win0 = np.clip(c - WIN_C // 2, 0, GRID_W - WIN_C)
    kc = np.arange(GRID_W)
    col_mask = (kc[None, :] >= win0[:, None]) & (kc[None, :] < win0[:, None] + WIN_C)
    dc_idx = np.clip(kc[None, :] - c[:, None] + WIN_C - 1, 0, 2 * WIN_C - 2)
    delta = np.arange(WIN_R)
    dr_idx = np.arange(WIN_R)[None, :] - delta[:, None] + WIN_R - 1
    t = rpb[:, dr_idx]
    t = t[:, :, :, dc_idx]
    t = jnp.where(col_mask[None, None, None], t, MASK_NEG)
    t = jnp.transpose(t, (0, 1, 3, 2, 4)).reshape(N_HEADS, WIN_R, GRID_W, WIN_R * GRID_W)
    t = t.reshape(N_HEADS // 2, 2, WIN_R, GRID_W, WIN_R * GRID_W)
    t = jnp.transpose(t, (0, 2, 1, 3, 4)).reshape(N_HEADS // 2, WIN_R, 2 * GRID_W, WIN_R * GRID_W)
    return t.astype(F32)


def _attn_body(q_ref, k_ref, v_ref, bias_ref, o_ref, *, rows):
    g = pl.program_id(2)
    win_tokens = WIN_R * GRID_W
    lane = lax.broadcasted_iota(jnp.int32, (GRID_W, 2 * HEAD_DIM), 1)
    first = lane < HEAD_DIM
    scale = HEAD_DIM ** -0.5

    def row(j, carry):
        r = g * ROWS_PER_ATTN_STEP + j
        r0 = jnp.clip(r - WIN_R // 2, 0, rows - WIN_R)
        delta = r - r0
        qoff = pl.multiple_of(j * GRID_W, GRID_W)
        koff = pl.multiple_of(r0 * GRID_W, GRID_W)
        q2 = q_ref[pl.ds(qoff, GRID_W), :]
        zero = jnp.zeros_like(q2)
        qm = jnp.concatenate([jnp.where(first, q2, zero), jnp.where(first, zero, q2)], axis=0)
        ks = k_ref[pl.ds(koff, win_tokens), :]
        vs = v_ref[pl.ds(koff, win_tokens), :]
        s = lax.dot_general(qm, ks, (((1,), (1,)), ((), ())), preferred_element_type=F32)
        s = s * scale + bias_ref[delta]
        m = jnp.max(s, axis=-1, keepdims=True)
        p = jnp.exp(s - m)
        denom = jnp.sum(p, axis=-1, keepdims=True)
        o = jnp.dot(p.astype(BF16), vs, preferred_element_type=F32) / denom
        o2 = jnp.where(first, o[:GRID_W], o[GRID_W:])
        o_ref[pl.ds(qoff, GRID_W), :] = o2.astype(BF16)
        return carry

    lax.fori_loop(0, ROWS_PER_ATTN_STEP, row, 0)


def _attention(q, k, v, bias_tab, batch, seq_len):
    n = q.shape[0]
    rows = seq_len // GRID_W
    tq = ROWS_PER_ATTN_STEP * GRID_W
    steps = seq_len // tq
    pair_w = 2 * HEAD_DIM
    return pl.pallas_call(
        functools.partial(_attn_body, rows=rows),
        grid=(N_HEADS // 2, batch, steps),
        in_specs=[
            pl.BlockSpec((tq, pair_w), lambda hp, b, g: (b * steps + g, hp)),
            pl.BlockSpec((seq_len, pair_w), lambda hp, b, g: (b, hp)),
            pl.BlockSpec((seq_len, pair_w), lambda hp, b, g: (b, hp)),
            pl.BlockSpec((None, WIN_R, 2 * GRID_W, WIN_R * GRID_W), lambda hp, b, g: (hp, 0, 0, 0)),
        ],
        out_specs=pl.BlockSpec((tq, pair_w), lambda hp, b, g: (b * steps + g, hp)),
        out_shape=jax.ShapeDtypeStruct((n, ATTN_CH), BF16),
        compiler_params=_cparams(("parallel", "parallel", "arbitrary")),
        name="nbr_attention",
    )(q, k, v, bias_tab)


def _outproj_body(h_ref, c_ref, a_ref, w_ref, b_ref, lng_ref, lnb_ref, *rest, alpha, with_router):
    if with_router:
        wr_ref, h1_ref, eidx_ref, gate_ref = rest
    else:
        (h1_ref,) = rest
    mix = jnp.dot(c_ref[...], w_ref[0:CONV_CH, :], preferred_element_type=F32)
    mix = mix + jnp.dot(a_ref[...], w_ref[CONV_CH:, :], preferred_element_type=F32)
    mix = mix + b_ref[...]
    h1 = _layer_norm(alpha * h_ref[...] + mix, lng_ref[...], lnb_ref[...])
    h1_ref[...] = h1
    if with_router:
        logits = lax.dot_general(wr_ref[...], h1, (((1,), (1,)), ((), ())),
                                 precision=lax.Precision.HIGHEST, preferred_element_type=F32)
        eid = lax.broadcasted_iota(jnp.int32, logits.shape, 0)
        m1 = jnp.max(logits, axis=0, keepdims=True)
        i1 = jnp.min(jnp.where(logits == m1, eid, N_EXPERTS), axis=0, keepdims=True)
        rest_l = jnp.where(eid == i1, -jnp.inf, logits)
        m2 = jnp.max(rest_l, axis=0, keepdims=True)
        i2 = jnp.min(jnp.where(rest_l == m2, eid, N_EXPERTS), axis=0, keepdims=True)
        e2 = jnp.exp(m2 - m1)
        den = 1.0 + e2
        eidx_ref[...] = jnp.concatenate([i1, i2], axis=0)
        gate_ref[...] = jnp.concatenate([1.0 / den, e2 / den], axis=0)


def _outproj(h, conv_out, attn_out, w, b, ln_g, ln_b, alpha, w_router_t=None):
    n, d = h.shape
    tm = TM_PROJ
    with_router = w_router_t is not None
    in_specs = [_row_spec(tm, d), _row_spec(tm, CONV_CH), _row_spec(tm, ATTN_CH),
                _const_spec(w.shape), _const_spec((1, d)), _const_spec((1, d)), _const_spec((1, d))]
    args = [h, conv_out, attn_out, w, b, ln_g, ln_b]
    out_shape = [jax.ShapeDtypeStruct((n, d), F32)]
    out_specs = [_row_spec(tm, d)]
    if with_router:
        in_specs.append(_const_spec(w_router_t.shape))
        args.append(w_router_t)
        out_shape += [jax.ShapeDtypeStruct((2, n), jnp.int32), jax.ShapeDtypeStruct((2, n), F32)]
        out_specs += [pl.BlockSpec((2, tm), lambda i: (0, i))] * 2
    res = pl.pallas_call(
        functools.partial(_outproj_body, alpha=alpha, with_router=with_router),
        grid=(n // tm,),
        in_specs=in_specs,
        out_specs=out_specs,
        out_shape=out_shape,
        compiler_params=_cparams(("parallel",)),
        name="outproj_router" if with_router else "outproj",
    )(*args)
    return res if with_router else res[0]


def _col_chunks(width, step):
    return [(lo, min(lo + step, width)) for lo in range(0, width, step)]


def _ple_and_norm(h1, f, p_ref, wg_ref, bg_ref, wp_ref, lng_ref, lnb_ref, alpha):
    hb = h1.astype(BF16)
    gate = jax.nn.sigmoid(jnp.dot(hb, wg_ref[...], preferred_element_type=F32) + bg_ref[...])
    e = gate * jnp.dot(p_ref[...].astype(BF16), wp_ref[...], preferred_element_type=F32)
    return _layer_norm(alpha * h1 + f + e, lng_ref[...], lnb_ref[...])


def _ffn_tail_body(h_ref, p_ref, w1_ref, w3_ref, w2_ref, wg_ref, bg_ref, wp_ref, lng_ref, lnb_ref, o_ref, *, alpha):
    h1 = h_ref[...]
    hb = h1.astype(BF16)
    f = jnp.zeros(h1.shape, F32)
    for lo, hi in _col_chunks(w1_ref.shape[1], 2 * MXU_N):
        a = jnp.dot(hb, w1_ref[:, lo:hi], preferred_element_type=F32)
        b = jnp.dot(hb, w3_ref[:, lo:hi], preferred_element_type=F32)
        mid = (a * jax.nn.sigmoid(a) * b).astype(BF16)
        f = f + jnp.dot(mid, w2_ref[lo:hi, :], preferred_element_type=F32)
    o_ref[...] = _ple_and_norm(h1, f, p_ref, wg_ref, bg_ref, wp_ref, lng_ref, lnb_ref, alpha)


def _ffn_tail(h1, p, w1, w3, w2, wg, bg, wp, ln_g, ln_b, alpha):
    n, d = h1.shape
    tm = TM_PROJ
    return pl.pallas_call(
        functools.partial(_ffn_tail_body, alpha=alpha),
        grid=(n // tm,),
        in_specs=[_row_spec(tm, d), _row_spec(tm, p.shape[1]),
                  _const_spec(w1.shape), _const_spec(w3.shape), _const_spec(w2.shape),
                  _const_spec(wg.shape), _const_spec((1, d)), _const_spec(wp.shape),
                  _const_spec((1, d)), _const_spec((1, d))],
        out_specs=_row_spec(tm, d),
        out_shape=jax.ShapeDtypeStruct((n, d), F32),
        compiler_params=_cparams(("parallel",)),
        name="ffn_tail",
    )(h1, p, w1, w3, w2, wg, bg, wp, ln_g, ln_b)


def _row_copy(src_ref, src_row, dst_ref, dst_row, sem):
    return pltpu.make_async_copy(src_ref.at[pl.ds(src_row, 1), :], dst_ref.at[pl.ds(dst_row, 1), :], sem)


def _load_slot_ids(dest_hbm, idx_smem, sem):
    n_ids = idx_smem.shape[0]
    start = pl.multiple_of(pl.program_id(0) * n_ids, n_ids)
    cp = pltpu.make_async_copy(dest_hbm.at[pl.ds(start, n_ids)], idx_smem, sem)
    cp.start()
    cp.wait()


def _dispatch_body(dest_hbm, h_ref, xs_in_hbm, xs_hbm, idx_smem, sem_idx, sem_rows):
    del xs_in_hbm
    tm = h_ref.shape[0]
    _load_slot_ids(dest_hbm, idx_smem, sem_idx)

    def issue(t, carry):
        _row_copy(h_ref, t, xs_hbm, idx_smem[2 * t], sem_rows).start()
        _row_copy(h_ref, t, xs_hbm, idx_smem[2 * t + 1], sem_rows).start()
        return carry

    lax.fori_loop(0, tm, issue, 0)

    def drain(t, carry):
        _row_copy(h_ref, t, xs_hbm, 0, sem_rows).wait()
        _row_copy(h_ref, t, xs_hbm, 0, sem_rows).wait()
        return carry

    lax.fori_loop(0, tm, drain, 0)


def _dispatch(h1, dest_flat, n_slots):
    n, d = h1.shape
    tm = TM_ROWDMA
    zeros = jnp.zeros((n_slots, d), F32)
    return pl.pallas_call(
        _dispatch_body,
        grid=(n // tm,),
        in_specs=[pl.BlockSpec(memory_space=pl.ANY), _row_spec(tm, d), pl.BlockSpec(memory_space=pl.ANY)],
        out_specs=pl.BlockSpec(memory_space=pl.ANY),
        out_shape=jax.ShapeDtypeStruct((n_slots, d), F32),
        scratch_shapes=[pltpu.SMEM((2 * tm,), jnp.int32), pltpu.SemaphoreType.DMA(()), pltpu.SemaphoreType.DMA(())],
        input_output_aliases={2: 0},
        compiler_params=_cparams(("arbitrary",)),
        name="moe_dispatch",
    )(dest_flat, h1, zeros)


def _moe_gemm_body(tile_e_ref, n_active_ref, x_ref, w1_ref, w3_ref, w2_ref, o_ref):
    del tile_e_ref
    i = pl.program_id(0)
    j = pl.program_id(1)

    @pl.when(j == 0)
    def _():
        o_ref[...] = jnp.zeros(o_ref.shape, F32)

    @pl.when(i < n_active_ref[0])
    def _():
        xb = x_ref[...].astype(BF16)
        a = jnp.dot(xb, w1_ref[...], preferred_element_type=F32)
        b = jnp.dot(xb, w3_ref[...], preferred_element_type=F32)
        mid = (a * jax.nn.sigmoid(a) * b).astype(BF16)
        o_ref[...] += jnp.dot(mid, w2_ref[...], preferred_element_type=F32)


def _moe_gemm(xs, w1, w3, w2, tile_e, n_active):
    n_slots, d = xs.shape
    f = w1.shape[2]
    tm, fc = TM_MOE, FC_MOE
    grid_spec = pltpu.PrefetchScalarGridSpec(
        num_scalar_prefetch=2,
        grid=(n_slots // tm, f // fc),
        in_specs=[
            pl.BlockSpec((tm, d), lambda i, j, te, na: (i, 0)),
            pl.BlockSpec((None, d, fc), lambda i, j, te, na: (te[i], 0, j)),
            pl.BlockSpec((None, d, fc), lambda i, j, te, na: (te[i], 0, j)),
            pl.BlockSpec((None, fc, d), lambda i, j, te, na: (te[i], j, 0)),
        ],
        out_specs=pl.BlockSpec((tm, d), lambda i, j, te, na: (i, 0)),
    )
    return pl.pallas_call(
        _moe_gemm_body,
        grid_spec=grid_spec,
        out_shape=jax.ShapeDtypeStruct((n_slots, d), F32),
        compiler_params=_cparams(("parallel", "arbitrary")),
        name="moe_gemm",
    )(tile_e, n_active, xs, w1, w3, w2)


def _moe_tail_body(dest_hbm, h_ref, p_ref, g_ref, y_hbm, wg_ref, bg_ref, wp_ref, lng_ref, lnb_ref, o_ref,
                   idx_smem, buf0, buf1, sem_idx, sem_rows, *, alpha):
    tm = h_ref.shape[0]
    _load_slot_ids(dest_hbm, idx_smem, sem_idx)

    def issue(t, carry):
        _row_copy(y_hbm, idx_smem[2 * t], buf0, t, sem_rows).start()
        _row_copy(y_hbm, idx_smem[2 * t + 1], buf1, t, sem_rows).start()
        return carry

    lax.fori_loop(0, tm, issue, 0)

    def drain(t, carry):
        _row_copy(y_hbm, 0, buf0, t, sem_rows).wait()
        _row_copy(y_hbm, 0, buf1, t, sem_rows).wait()
        return carry

    lax.fori_loop(0, tm, drain, 0)
    g = g_ref[...]
    f = buf0[...] * g[:, 0:1] + buf1[...] * g[:, 1:2]
    o_ref[...] = _ple_and_norm(h_ref[...], f, p_ref, wg_ref, bg_ref, wp_ref, lng_ref, lnb_ref, alpha)


def _moe_tail(h1, p, gates_t, y, dest_flat, wg, bg, wp, ln_g, ln_b, alpha):
    n, d = h1.shape
    tm = TM_ROWDMA
    return pl.pallas_call(
        functools.partial(_moe_tail_body, alpha=alpha),
        grid=(n // tm,),
        in_specs=[pl.BlockSpec(memory_space=pl.ANY), _row_spec(tm, d), _row_spec(tm, p.shape[1]),
                  _row_spec(tm, 2), pl.BlockSpec(memory_space=pl.ANY),
                  _const_spec(wg.shape), _const_spec((1, d)), _const_spec(wp.shape),
                  _const_spec((1, d)), _const_spec((1, d))],
        out_specs=_row_spec(tm, d),
        out_shape=jax.ShapeDtypeStruct((n, d), F32),
        scratch_shapes=[pltpu.SMEM((2 * tm,), jnp.int32), pltpu.VMEM((tm, d), F32), pltpu.VMEM((tm, d), F32),
                        pltpu.SemaphoreType.DMA(()), pltpu.SemaphoreType.DMA(())],
        compiler_params=_cparams(("arbitrary",)),
        name="moe_tail",
    )(dest_flat, h1, p, gates_t, y, wg, bg, wp, ln_g, ln_b)


def _moe_plan(eidx, n_tokens):
    tm = TM_MOE
    e_flat = eidx.T.reshape(-1)
    onehot = (e_flat[:, None] == jnp.arange(N_EXPERTS, dtype=jnp.int32)[None, :]).astype(jnp.int32)
    csum = jnp.cumsum(onehot, axis=0)
    rank = jnp.sum(csum * onehot, axis=1) - 1
    counts = csum[-1]
    padded = (counts + tm - 1) // tm * tm
    pends = jnp.cumsum(padded)
    pstarts = pends - padded
    dest = (jnp.sum(pstarts[None, :] * onehot, axis=1) + rank).astype(jnp.int32)
    n_tiles = (2 * n_tokens) // tm + N_EXPERTS
    tile_start = jnp.arange(n_tiles, dtype=jnp.int32) * tm
    tile_e = jnp.minimum(jnp.sum((tile_start[:, None] >= pends[None, :]).astype(jnp.int32), axis=1),
                         N_EXPERTS - 1).astype(jnp.int32)
    n_active = (pends[-1] // tm).astype(jnp.int32).reshape(1)
    return dest, tile_e, n_active, n_tiles * tm


def kernel(x, p, ln_in_g, ln_in_b, w_in, b_in, conv_w, conv_b, conv_ln_g, conv_ln_b, rpb, w_out, b_out,
           ln1_g, ln1_b, ffn_w1, ffn_w3, ffn_w2, w_router, moe_w1, moe_w3, moe_w2, ple_w, ple_gate_w,
           ple_gate_b, ln2_g, ln2_b):
    batch, seq_len, d = x.shape
    depth = w_in.shape[0]
    n = batch * seq_len
    alpha = (2.0 * depth) ** 0.25
    row = lambda v: v.reshape(1, -1)

    h = x.reshape(n, d)
    for i in range(depth):
        w_in_i = w_in[i].astype(BF16)
        if i == 0:
            h, u, q, k, v = _inproj(h, row(ln_in_g), row(ln_in_b), w_in_i, row(b_in[i]), pre_ln=True)
        else:
            u, q, k, v = _inproj(h, row(ln_in_g), row(ln_in_b), w_in_i, row(b_in[i]), pre_ln=False)
        conv_out = _conv_module(u, conv_w[i], row(conv_b[i]), row(conv_ln_g[i]), row(conv_ln_b[i]), seq_len)
        attn_out = _attention(q, k, v, _attn_bias_table(rpb[i]), batch, seq_len)
        j = i // 2
        is_moe = i % 2 == 1
        w_out_i = w_out[i].astype(BF16)
        wg = ple_gate_w[i].astype(BF16)
        wp = ple_w[i].astype(BF16)
        p_i = p[i].reshape(n, -1)
        if not is_moe:
            h1 = _outproj(h, conv_out, attn_out, w_out_i, row(b_out[i]), row(ln1_g[i]), row(ln1_b[i]), alpha)
            h = _ffn_tail(h1, p_i, ffn_w1[j].astype(BF16), ffn_w3[j].astype(BF16), ffn_w2[j].astype(BF16),
                          wg, row(ple_gate_b[i]), wp, row(ln2_g[i]), row(ln2_b[i]), alpha)
        else:
            h1, eidx, gates = _outproj(h, conv_out, attn_out, w_out_i, row(b_out[i]), row(ln1_g[i]),
                                       row(ln1_b[i]), alpha, w_router_t=w_router[j].T)
            dest, tile_e, n_active, n_slots = _moe_plan(eidx, n)
            xs = _dispatch(h1, dest, n_slots)
            y = _moe_gemm(xs, moe_w1[j].astype(BF16), moe_w3[j].astype(BF16), moe_w2[j].astype(BF16),
                          tile_e, n_active)
            h = _moe_tail(h1, p_i, gates.T, y, dest, wg, row(ple_gate_b[i]), wp, row(ln2_g[i]), row(ln2_b[i]),
                          alpha)
    return h.reshape(batch, seq_len, d)
```

```python
import collections
import functools

import numpy as np
import jax
import jax.numpy as jnp
from jax import lax
from jax.experimental import pallas as pl
from jax.experimental.pallas import tpu as pltpu

F32 = jnp.float32
BF16 = jnp.bfloat16

CONV_CH = 512
N_HEADS = 8
HEAD_DIM = 64
ATTN_CH = N_HEADS * HEAD_DIM
CONV_K = 31
GRID_W = 64
WIN_R = 8
WIN_C = 16
N_EXPERTS = 8
LN_EPS = 1e-5
MASK_NEG = -1e30

LANES = 128
SUBLANES = 8
MXU_N = 256
VMEM_LIMIT = 56 * 1024 * 1024

TM_PROJ = 512
TT_CONV = 512
CONV_HALO = 16
CONV_CHUNK = 32
ROWS_PER_ATTN_STEP = 8
TM_MOE = 512
FC_MOE = 1792
TM_ROWDMA = 512
ROWDMA_UNROLL = 8


def _cparams(semantics):
    return pltpu.CompilerParams(dimension_semantics=semantics, vmem_limit_bytes=VMEM_LIMIT)


def _layer_norm(x, g, b):
    mu = jnp.mean(x, axis=-1, keepdims=True)
    xc = x - mu
    var = jnp.mean(xc * xc, axis=-1, keepdims=True)
    return xc * lax.rsqrt(var + LN_EPS) * g + b


class _Layered(collections.namedtuple("_Layered", ["stack", "layer"])):
    @property
    def shape(self):
        return self.stack.shape[1:]


def _param_spec(x):
    nd = len(x.shape)
    if isinstance(x, _Layered):
        layer = x.layer
        return pl.BlockSpec((None,) + tuple(x.shape), lambda *_: (layer,) + (0,) * nd,
                            pipeline_mode=pl.Buffered(1))
    return pl.BlockSpec(tuple(x.shape), lambda *_: (0,) * nd, pipeline_mode=pl.Buffered(1))


def _arr(x):
    return x.stack if isinstance(x, _Layered) else x


def _row_spec(tm, width):
    return pl.BlockSpec((tm, width), lambda i: (i, 0))


def _inproj_body(x_ref, lng_ref, lnb_ref, w_ref, b_ref, *out_refs, pre_ln):
    if pre_ln:
        h_ref, u_ref, q_ref, k_ref, v_ref = out_refs
    else:
        u_ref, q_ref, k_ref, v_ref = out_refs
    x = x_ref[...]
    if pre_ln:
        x = _layer_norm(x, lng_ref[...], lnb_ref[...])
        h_ref[...] = x
    xb = x.astype(BF16)

    def proj(lo, hi):
        return jnp.dot(xb, w_ref[:, lo:hi], preferred_element_type=F32) + b_ref[:, lo:hi]

    a = proj(0, CONV_CH)
    g = proj(CONV_CH, 2 * CONV_CH)
    u_ref[...] = a * jax.nn.sigmoid(g)
    base = 2 * CONV_CH
    q_ref[...] = proj(base, base + ATTN_CH).astype(BF16)
    k_ref[...] = proj(base + ATTN_CH, base + 2 * ATTN_CH).astype(BF16)
    v_ref[...] = proj(base + 2 * ATTN_CH, base + 3 * ATTN_CH).astype(BF16)


def _inproj(x, ln_g, ln_b, w, b, pre_ln):
    n, d = x.shape
    tm = TM_PROJ
    out_shape = [jax.ShapeDtypeStruct((n, CONV_CH), F32)] + [jax.ShapeDtypeStruct((n, ATTN_CH), BF16)] * 3
    out_specs = [_row_spec(tm, CONV_CH)] + [_row_spec(tm, ATTN_CH)] * 3
    if pre_ln:
        out_shape = [jax.ShapeDtypeStruct((n, d), F32)] + out_shape
        out_specs = [_row_spec(tm, d)] + out_specs
    params = [ln_g, ln_b, w, b]
    return pl.pallas_call(
        functools.partial(_inproj_body, pre_ln=pre_ln),
        grid=(n // tm,),
        in_specs=[_row_spec(tm, d)] + [_param_spec(a) for a in params],
        out_specs=out_specs,
        out_shape=out_shape,
        compiler_params=_cparams(("parallel",)),
        name="inproj_ln" if pre_ln else "inproj",
    )(x, *[_arr(a) for a in params])


def _conv_body(prev_ref, cur_ref, next_ref, w_ref, b_ref, lng_ref, lnb_ref, o_ref, ext_ref, sh_ref, acc_ref, *,
               tiles_per_seq):
    i = pl.program_id(0)
    tt = cur_ref.shape[0]
    pos = i % tiles_per_seq
    zeros = jnp.zeros((CONV_HALO, CONV_CH), F32)
    ext_ref[0:CONV_HALO, :] = jnp.where(pos > 0, prev_ref[...], zeros)
    ext_ref[CONV_HALO:CONV_HALO + tt, :] = cur_ref[...]
    ext_ref[CONV_HALO + tt:, :] = jnp.where(pos < tiles_per_seq - 1, next_ref[...], zeros)
    span = sh_ref.shape[1]
    for b in range(1, SUBLANES):
        sh_ref[b - 1] = ext_ref[b:b + span, :]
    shift = CONV_HALO - CONV_K // 2

    def chunk(c, carry):
        base = pl.multiple_of(c * CONV_CHUNK, CONV_CHUNK)
        groups = CONV_CHUNK // SUBLANES
        acc = jnp.zeros((groups, SUBLANES, CONV_CH), F32)
        for k in range(CONV_K):
            b, a = (k + shift) % SUBLANES, (k + shift) // SUBLANES
            rows = pl.ds(base + a * SUBLANES, CONV_CHUNK)
            tap = ext_ref[rows, :] if b == 0 else sh_ref[b - 1, rows, :]
            acc = acc + tap.reshape(groups, SUBLANES, CONV_CH) * w_ref[k][None]
        acc_ref[pl.ds(base, CONV_CHUNK), :] = acc.reshape(CONV_CHUNK, CONV_CH)
        return carry

    lax.fori_loop(0, tt // CONV_CHUNK, chunk, 0)
    y = _layer_norm(acc_ref[...] + b_ref[...], lng_ref[...], lnb_ref[...])
    o_ref[...] = (y * jax.nn.sigmoid(y)).astype(BF16)


def _conv_module(u, w_rep, b, ln_g, ln_b, seq_len):
    n = u.shape[0]
    tt = TT_CONV
    hb = tt // CONV_HALO
    n_halo_blocks = n // CONV_HALO
    params = [w_rep, b, ln_g, ln_b]
    return pl.pallas_call(
        functools.partial(_conv_body, tiles_per_seq=seq_len // tt),
        grid=(n // tt,),
        in_specs=[
            pl.BlockSpec((CONV_HALO, CONV_CH), lambda i: (jnp.maximum(i * hb - 1, 0), 0)),
            pl.BlockSpec((tt, CONV_CH), lambda i: (i, 0)),
            pl.BlockSpec((CONV_HALO, CONV_CH), lambda i: (jnp.minimum((i + 1) * hb, n_halo_blocks - 1), 0)),
        ] + [_param_spec(a) for a in params],
        out_specs=pl.BlockSpec((tt, CONV_CH), lambda i: (i, 0)),
        out_shape=jax.ShapeDtypeStruct((n, CONV_CH), BF16),
        scratch_shapes=[pltpu.VMEM((tt + 2 * CONV_HALO, CONV_CH), F32),
                        pltpu.VMEM((SUBLANES - 1, tt + 2 * CONV_HALO - SUBLANES, CONV_CH), F32),
                        pltpu.VMEM((tt, CONV_CH), F32)],
        compiler_params=_cparams(("parallel",)),
        name="conv_module",
    )(u, u, u, *[_arr(a) for a in params])


def _attn_bias_tables(rpb):
    c = np.arange(GRID_W)
    kc = np.arange(GRID_W)
    win0 = np.clip(c - WIN_C // 2, 0, GRID_W - WIN_C)
    col_mask = (kc[None, :] >= win0[:, None]) & (kc[None, :] < win0[:, None] + WIN_C)
    dc_idx = np.clip(kc[None, :] - c[:, None] + WIN_C - 1, 0, 2 * WIN_C - 2)
    sel_c = (dc_idx[:, :, None] == np.arange(2 * WIN_C - 1)[None, None, :]) & col_mask[:, :, None]
    delta = np.arange(WIN_R)
    dr_idx = np.arange(WIN_R)[None, :] - delta[:, None] + WIN_R - 1
    sel_r = dr_idx[:, :, None] == np.arange(2 * WIN_R - 1)[None, None, :]
    t = jnp.einsum("lhrd,xir,ckd->lhxcik", rpb, jnp.asarray(sel_r, F32), jnp.asarray(sel_c, F32),
                   precision=lax.Precision.HIGHEST)
    t = jnp.where(col_mask[None, None, None, :, None, :], t, MASK_NEG)
    n_layers = rpb.shape[0]
    t = t.reshape(n_layers, N_HEADS // 2, 2, WIN_R, GRID_W, WIN_R * GRID_W)
    t = jnp.transpose(t, (0, 1, 3, 2, 4, 5))
    return t.reshape(n_layers, N_HEADS // 2, WIN_R, 2 * GRID_W, WIN_R * GRID_W).astype(F32)


def _attn_body(q_ref, k_ref, v_ref, bias_ref, o_ref, s_ref, p_ref, l_ref, *, rows):
    g = pl.program_id(2)
    win_tokens = WIN_R * GRID_W
    lane = lax.broadcasted_iota(jnp.int32, (GRID_W, 2 * HEAD_DIM), 1)
    first = lane < HEAD_DIM
    scale = HEAD_DIM ** -0.5

    koffs = []
    for j in range(ROWS_PER_ATTN_STEP):
        r = g * ROWS_PER_ATTN_STEP + j
        r0 = jnp.clip(r - WIN_R // 2, 0, rows - WIN_R)
        koff = pl.multiple_of(r0 * GRID_W, GRID_W)
        koffs.append(koff)
        q2 = q_ref[j * GRID_W:(j + 1) * GRID_W, :] * scale
        zero = jnp.zeros_like(q2)
        qm = jnp.concatenate([jnp.where(first, q2, zero), jnp.where(first, zero, q2)], axis=0)
        ks = k_ref[pl.ds(koff, win_tokens), :]
        s = lax.dot_general(qm, ks, (((1,), (1,)), ((), ())), preferred_element_type=F32)
        s_ref[j] = s + bias_ref[r - r0]
    for j in range(ROWS_PER_ATTN_STEP):
        s = s_ref[j]
        m = jnp.max(s, axis=-1, keepdims=True)
        p = jnp.exp(s - m)
        l_ref[j] = jnp.broadcast_to(jnp.sum(p, axis=-1, keepdims=True), l_ref.shape[1:])
        p_ref[j] = p.astype(BF16)
    for j in range(ROWS_PER_ATTN_STEP):
        vs = v_ref[pl.ds(koffs[j], win_tokens), :]
        o = jnp.dot(p_ref[j], vs, preferred_element_type=F32) / l_ref[j]
        o2 = jnp.where(first, o[:GRID_W], o[GRID_W:])
        o_ref[j * GRID_W:(j + 1) * GRID_W, :] = o2.astype(BF16)


def _attention(q, k, v, bias_tabs, layer, batch, seq_len):
    n = q.shape[0]
    rows = seq_len // GRID_W
    tq = ROWS_PER_ATTN_STEP * GRID_W
    steps = seq_len // tq
    pair_w = 2 * HEAD_DIM
    return pl.pallas_call(
        functools.partial(_attn_body, rows=rows),
        grid=(N_HEADS // 2, batch, steps),
        in_specs=[
            pl.BlockSpec((tq, pair_w), lambda hp, b, g: (b * steps + g, hp)),
            pl.BlockSpec((seq_len, pair_w), lambda hp, b, g: (b, hp)),
            pl.BlockSpec((seq_len, pair_w), lambda hp, b, g: (b, hp)),
            pl.BlockSpec((None, None, WIN_R, 2 * GRID_W, WIN_R * GRID_W), lambda hp, b, g: (layer, hp, 0, 0, 0)),
        ],
        out_specs=pl.BlockSpec((tq, pair_w), lambda hp, b, g: (b * steps + g, hp)),
        out_shape=jax.ShapeDtypeStruct((n, ATTN_CH), BF16),
        scratch_shapes=[pltpu.VMEM((ROWS_PER_ATTN_STEP, 2 * GRID_W, WIN_R * GRID_W), F32),
                        pltpu.VMEM((ROWS_PER_ATTN_STEP, 2 * GRID_W, WIN_R * GRID_W), BF16),
                        pltpu.VMEM((ROWS_PER_ATTN_STEP, 2 * GRID_W, pair_w), F32)],
        compiler_params=_cparams(("parallel", "parallel", "arbitrary")),
        name="nbr_attention",
    )(q, k, v, bias_tabs)


def _outproj_body(h_ref, c_ref, a_ref, w_ref, b_ref, lng_ref, lnb_ref, *rest, alpha, with_router):
    if with_router:
        wr_ref, h1_ref, eidx_ref, gate_ref = rest
    else:
        (h1_ref,) = rest
    mix = jnp.dot(c_ref[...], w_ref[0:CONV_CH, :], preferred_element_type=F32)
    mix = mix + jnp.dot(a_ref[...], w_ref[CONV_CH:, :], preferred_element_type=F32)
    mix = mix + b_ref[...]
    h1 = _layer_norm(alpha * h_ref[...] + mix, lng_ref[...], lnb_ref[...])
    h1_ref[...] = h1
    if with_router:
        logits = lax.dot_general(wr_ref[...], h1, (((1,), (1,)), ((), ())),
                                 precision=lax.Precision.HIGHEST, preferred_element_type=F32)
        eid = lax.broadcasted_iota(jnp.int32, logits.shape, 0)
        m1 = jnp.max(logits, axis=0, keepdims=True)
        i1 = jnp.min(jnp.where(logits == m1, eid, N_EXPERTS), axis=0, keepdims=True)
        rest_l = jnp.where(eid == i1, -jnp.inf, logits)
        m2 = jnp.max(rest_l, axis=0, keepdims=True)
        i2 = jnp.min(jnp.where(rest_l == m2, eid, N_EXPERTS), axis=0, keepdims=True)
        e2 = jnp.exp(m2 - m1)
        den = 1.0 + e2
        eidx_ref[...] = jnp.concatenate([i1, i2], axis=0)
        gate_ref[...] = jnp.concatenate([1.0 / den, e2 / den], axis=0)


def _outproj(h, conv_out, attn_out, w, b, ln_g, ln_b, alpha, w_router_t=None):
    n, d = h.shape
    tm = TM_PROJ
    with_router = w_router_t is not None
    params = [w, b, ln_g, ln_b] + ([w_router_t] if with_router else [])
    out_shape = [jax.ShapeDtypeStruct((n, d), F32)]
    out_specs = [_row_spec(tm, d)]
    if with_router:
        out_shape += [jax.ShapeDtypeStruct((2, n), jnp.int32), jax.ShapeDtypeStruct((2, n), F32)]
        out_specs += [pl.BlockSpec((2, tm), lambda i: (0, i))] * 2
    res = pl.pallas_call(
        functools.partial(_outproj_body, alpha=alpha, with_router=with_router),
        grid=(n // tm,),
        in_specs=[_row_spec(tm, d), _row_spec(tm, CONV_CH), _row_spec(tm, ATTN_CH)]
        + [_param_spec(a) for a in params],
        out_specs=out_specs,
        out_shape=out_shape,
        compiler_params=_cparams(("parallel",)),
        name="outproj_router" if with_router else "outproj",
    )(h, conv_out, attn_out, *[_arr(a) for a in params])
    return res if with_router else res[0]


def _col_chunks(width, step):
    return [(lo, min(lo + step, width)) for lo in range(0, width, step)]


def _ple_and_norm(h1, f, p_ref, wg_ref, bg_ref, wp_ref, lng_ref, lnb_ref, alpha):
    hb = h1.astype(BF16)
    gate = jax.nn.sigmoid(jnp.dot(hb, wg_ref[...], preferred_element_type=F32) + bg_ref[...])
    e = gate * jnp.dot(p_ref[...].astype(BF16), wp_ref[...], preferred_element_type=F32)
    return _layer_norm(alpha * h1 + f + e, lng_ref[...], lnb_ref[...])


def _layer_rows_spec(tm, stack, layer):
    return pl.BlockSpec((None, tm, stack.shape[2]), lambda i: (layer, i, 0))


def _ffn_tail_body(h_ref, p_ref, w1_ref, w3_ref, w2_ref, wg_ref, bg_ref, wp_ref, lng_ref, lnb_ref, o_ref, *, alpha):
    h1 = h_ref[...]
    hb = h1.astype(BF16)
    f = jnp.zeros(h1.shape, F32)
    for lo, hi in _col_chunks(w1_ref.shape[1], 2 * MXU_N):
        a = jnp.dot(hb, w1_ref[:, lo:hi], preferred_element_type=F32)
        b = jnp.dot(hb, w3_ref[:, lo:hi], preferred_element_type=F32)
        mid = (a * jax.nn.sigmoid(a) * b).astype(BF16)
        f = f + jnp.dot(mid, w2_ref[lo:hi, :], preferred_element_type=F32)
    o_ref[...] = _ple_and_norm(h1, f, p_ref, wg_ref, bg_ref, wp_ref, lng_ref, lnb_ref, alpha)


def _ffn_tail(h1, p_stack, layer, w1, w3, w2, wg, bg, wp, ln_g, ln_b, alpha):
    n, d = h1.shape
    tm = TM_PROJ
    params = [w1, w3, w2, wg, bg, wp, ln_g, ln_b]
    return pl.pallas_call(
        functools.partial(_ffn_tail_body, alpha=alpha),
        grid=(n // tm,),
        in_specs=[_row_spec(tm, d), _layer_rows_spec(tm, p_stack, layer)] + [_param_spec(a) for a in params],
        out_specs=_row_spec(tm, d),
        out_shape=jax.ShapeDtypeStruct((n, d), F32),
        compiler_params=_cparams(("parallel",)),
        name="ffn_tail",
    )(h1, p_stack, *[_arr(a) for a in params])


def _row_copy(src_ref, src_row, dst_ref, dst_row, sem):
    return pltpu.make_async_copy(src_ref.at[pl.ds(src_row, 1), :], dst_ref.at[pl.ds(dst_row, 1), :], sem)


def _slot_ids_copy(dest_hbm, idx_smem, sems, step):
    n_ids = idx_smem.shape[0] // 2
    start = pl.multiple_of(step * n_ids, n_ids)
    half = step % 2
    dst = idx_smem.at[pl.ds(pl.multiple_of(half * n_ids, n_ids), n_ids)]
    return pltpu.make_async_copy(dest_hbm.at[pl.ds(start, n_ids)], dst, sems.at[half])


def _for_token_groups(tm, fn):
    def group(o, carry):
        base = pl.multiple_of(o * ROWDMA_UNROLL, ROWDMA_UNROLL)
        for r in range(ROWDMA_UNROLL):
            fn(base + r)
        return carry

    lax.fori_loop(0, tm // ROWDMA_UNROLL, group, 0)


def _dispatch_body(dest_hbm, h_ref, xs_in_hbm, xs_hbm, idx_smem, sem_idx, sem_rows):
    del xs_in_hbm
    i = pl.program_id(0)
    n_steps = pl.num_programs(0)
    tm = h_ref.shape[0]
    half = i % 2

    @pl.when(i == 0)
    def _():
        _slot_ids_copy(dest_hbm, idx_smem, sem_idx, i).start()

    @pl.when(i + 1 < n_steps)
    def _():
        _slot_ids_copy(dest_hbm, idx_smem, sem_idx, i + 1).start()

    _slot_ids_copy(dest_hbm, idx_smem, sem_idx, i).wait()

    ids0 = half * (2 * tm)

    def issue(t):
        _row_copy(h_ref, t, xs_hbm, idx_smem[ids0 + 2 * t], sem_rows).start()
        _row_copy(h_ref, t, xs_hbm, idx_smem[ids0 + 2 * t + 1], sem_rows).start()

    def drain(t):
        _row_copy(h_ref, t, xs_hbm, 0, sem_rows).wait()
        _row_copy(h_ref, t, xs_hbm, 0, sem_rows).wait()

    _for_token_groups(tm, issue)
    _for_token_groups(tm, drain)


def _dispatch(h1, dest_flat, n_slots):
    n, d = h1.shape
    tm = TM_ROWDMA
    zeros = jnp.zeros((n_slots, d), F32)
    return pl.pallas_call(
        _dispatch_body,
        grid=(n // tm,),
        in_specs=[pl.BlockSpec(memory_space=pl.ANY), _row_spec(tm, d), pl.BlockSpec(memory_space=pl.ANY)],
        out_specs=pl.BlockSpec(memory_space=pl.ANY),
        out_shape=jax.ShapeDtypeStruct((n_slots, d), F32),
        scratch_shapes=[pltpu.SMEM((4 * tm,), jnp.int32), pltpu.SemaphoreType.DMA((2,)),
                        pltpu.SemaphoreType.DMA(())],
        input_output_aliases={2: 0},
        compiler_params=_cparams(("arbitrary",)),
        name="moe_dispatch",
    )(dest_flat, h1, zeros)


def _moe_gemm_body(tile_e_ref, n_active_ref, x_ref, w1_ref, w3_ref, w2_ref, o_ref):
    del tile_e_ref
    i = pl.program_id(0)
    j = pl.program_id(1)

    @pl.when(j == 0)
    def _():
        o_ref[...] = jnp.zeros(o_ref.shape, F32)

    @pl.when(i < n_active_ref[0])
    def _():
        xb = x_ref[...].astype(BF16)
        a = jnp.dot(xb, w1_ref[...], preferred_element_type=F32)
        b = jnp.dot(xb, w3_ref[...], preferred_element_type=F32)
        mid = (a * jax.nn.sigmoid(a) * b).astype(BF16)
        o_ref[...] += jnp.dot(mid, w2_ref[...], preferred_element_type=F32)


def _moe_gemm(xs, w1, w3, w2, first_expert, tile_e, n_active):
    n_slots, d = xs.shape
    f = w1.shape[2]
    tm, fc = TM_MOE, FC_MOE
    grid_spec = pltpu.PrefetchScalarGridSpec(
        num_scalar_prefetch=2,
        grid=(n_slots // tm, f // fc),
        in_specs=[
            pl.BlockSpec((tm, d), lambda i, j, te, na: (i, 0)),
            pl.BlockSpec((None, d, fc), lambda i, j, te, na: (first_expert + te[i], 0, j)),
            pl.BlockSpec((None, d, fc), lambda i, j, te, na: (first_expert + te[i], 0, j)),
            pl.BlockSpec((None, fc, d), lambda i, j, te, na: (first_expert + te[i], j, 0)),
        ],
        out_specs=pl.BlockSpec((tm, d), lambda i, j, te, na: (i, 0)),
    )
    return pl.pallas_call(
        _moe_gemm_body,
        grid_spec=grid_spec,
        out_shape=jax.ShapeDtypeStruct((n_slots, d), F32),
        compiler_params=_cparams(("parallel", "arbitrary")),
        name="moe_gemm",
    )(tile_e, n_active, xs, w1, w3, w2)


def _moe_tail_body(dest_hbm, h_ref, p_ref, g_ref, y_hbm, wg_ref, bg_ref, wp_ref, lng_ref, lnb_ref, o_ref,
                   idx_smem, buf, sem_idx, sem_rows, *, alpha):
    i = pl.program_id(0)
    n_steps = pl.num_programs(0)
    tm = h_ref.shape[0]
    half = i % 2

    def issue_gathers(step):
        hs = step % 2
        ids0 = hs * (2 * tm)

        def issue(t):
            _row_copy(y_hbm, idx_smem[ids0 + 2 * t], buf.at[hs, 0], t, sem_rows.at[hs]).start()
            _row_copy(y_hbm, idx_smem[ids0 + 2 * t + 1], buf.at[hs, 1], t, sem_rows.at[hs]).start()

        _for_token_groups(tm, issue)

    @pl.when(i == 0)
    def _():
        cp = _slot_ids_copy(dest_hbm, idx_smem, sem_idx, i)
        cp.start()
        cp.wait()
        issue_gathers(i)

    @pl.when(i + 1 < n_steps)
    def _():
        cp = _slot_ids_copy(dest_hbm, idx_smem, sem_idx, i + 1)
        cp.start()
        cp.wait()
        issue_gathers(i + 1)

    def drain(t):
        _row_copy(y_hbm, 0, buf.at[half, 0], t, sem_rows.at[half]).wait()
        _row_copy(y_hbm, 0, buf.at[half, 1], t, sem_rows.at[half]).wait()

    _for_token_groups(tm, drain)
    g = g_ref[...]
    f = buf[half, 0] * g[:, 0:1] + buf[half, 1] * g[:, 1:2]
    o_ref[...] = _ple_and_norm(h_ref[...], f, p_ref, wg_ref, bg_ref, wp_ref, lng_ref, lnb_ref, alpha)


def _moe_tail(h1, p_stack, layer, gates_t, y, dest_flat, wg, bg, wp, ln_g, ln_b, alpha):
    n, d = h1.shape
    tm = TM_ROWDMA
    params = [wg, bg, wp, ln_g, ln_b]
    return pl.pallas_call(
        functools.partial(_moe_tail_body, alpha=alpha),
        grid=(n // tm,),
        in_specs=[pl.BlockSpec(memory_space=pl.ANY), _row_spec(tm, d), _layer_rows_spec(tm, p_stack, layer),
                  _row_spec(tm, 2), pl.BlockSpec(memory_space=pl.ANY)] + [_param_spec(a) for a in params],
        out_specs=_row_spec(tm, d),
        out_shape=jax.ShapeDtypeStruct((n, d), F32),
        scratch_shapes=[pltpu.SMEM((4 * tm,), jnp.int32), pltpu.VMEM((2, 2, tm, d), F32),
                        pltpu.SemaphoreType.DMA((2,)), pltpu.SemaphoreType.DMA((2,))],
        compiler_params=_cparams(("arbitrary",)),
        name="moe_tail",
    )(dest_flat, h1, p_stack, gates_t, y, *[_arr(a) for a in params])


def _moe_plan(eidx, n_tokens):
    tm = TM_MOE
    e_flat = eidx.T.reshape(-1)
    onehot = (e_flat[:, None] == jnp.arange(N_EXPERTS, dtype=jnp.int32)[None, :]).astype(jnp.int32)
    csum = jnp.cumsum(onehot, axis=0)
    rank = jnp.sum(csum * onehot, axis=1) - 1
    counts = csum[-1]
    padded = (counts + tm - 1) // tm * tm
    pends = jnp.cumsum(padded)
    pstarts = pends - padded
    dest = (jnp.sum(pstarts[None, :] * onehot, axis=1) + rank).astype(jnp.int32)
    n_tiles = (2 * n_tokens) // tm + N_EXPERTS
    tile_start = jnp.arange(n_tiles, dtype=jnp.int32) * tm
    tile_e = jnp.minimum(jnp.sum((tile_start[:, None] >= pends[None, :]).astype(jnp.int32), axis=1),
                         N_EXPERTS - 1).astype(jnp.int32)
    n_active = (pends[-1] // tm).astype(jnp.int32).reshape(1)
    return dest, tile_e, n_active, n_tiles * tm


def kernel(x, p, ln_in_g, ln_in_b, w_in, b_in, conv_w, conv_b, conv_ln_g, conv_ln_b, rpb, w_out, b_out,
           ln1_g, ln1_b, ffn_w1, ffn_w3, ffn_w2, w_router, moe_w1, moe_w3, moe_w2, ple_w, ple_gate_w,
           ple_gate_b, ln2_g, ln2_b):
    batch, seq_len, d = x.shape
    depth = w_in.shape[0]
    n = batch * seq_len
    alpha = (2.0 * depth) ** 0.25

    rows = lambda v: v.reshape(v.shape[0], 1, v.shape[-1])
    ln_in_g, ln_in_b = ln_in_g.reshape(1, 1, d), ln_in_b.reshape(1, 1, d)
    w_in_b, w_out_b = w_in.astype(BF16), w_out.astype(BF16)
    ffn_w1_b, ffn_w3_b, ffn_w2_b = ffn_w1.astype(BF16), ffn_w3.astype(BF16), ffn_w2.astype(BF16)
    wg_b, wp_b = ple_gate_w.astype(BF16), ple_w.astype(BF16)
    moe_shape = lambda w: w.reshape((-1,) + w.shape[2:])
    moe_w1_b, moe_w3_b, moe_w2_b = (moe_shape(w).astype(BF16) for w in (moe_w1, moe_w3, moe_w2))
    b_in, conv_b, conv_ln_g, conv_ln_b = rows(b_in), rows(conv_b), rows(conv_ln_g), rows(conv_ln_b)
    b_out, ln1_g, ln1_b, ple_gate_b, ln2_g, ln2_b = (rows(v) for v in (b_out, ln1_g, ln1_b, ple_gate_b, ln2_g, ln2_b))
    conv_w_rep = jnp.broadcast_to(conv_w[:, :, None, :], conv_w.shape[:2] + (SUBLANES, conv_w.shape[2]))
    bias_tabs = _attn_bias_tables(rpb)
    p_stack = p.reshape(depth, n, p.shape[-1])

    h = x.reshape(n, d)
    for i in range(depth):
        L = lambda stack, layer=i: _Layered(stack, layer)
        if i == 0:
            h, u, q, k, v = _inproj(h, L(ln_in_g, 0), L(ln_in_b, 0), L(w_in_b), L(b_in), pre_ln=True)
        else:
            u, q, k, v = _inproj(h, L(ln_in_g, 0), L(ln_in_b, 0), L(w_in_b), L(b_in), pre_ln=False)
        conv_out = _conv_module(u, L(conv_w_rep), L(conv_b), L(conv_ln_g), L(conv_ln_b), seq_len)
        attn_out = _attention(q, k, v, bias_tabs, i, batch, seq_len)
        j = i // 2
        tail_params = (L(wg_b), L(ple_gate_b), L(wp_b), L(ln2_g), L(ln2_b))
        if i % 2 == 0:
            h1 = _outproj(h, conv_out, attn_out, L(w_out_b), L(b_out), L(ln1_g), L(ln1_b), alpha)
            h = _ffn_tail(h1, p_stack, i, L(ffn_w1_b, j), L(ffn_w3_b, j), L(ffn_w2_b, j), *tail_params, alpha)
        else:
            h1, eidx, gates = _outproj(h, conv_out, attn_out, L(w_out_b), L(b_out), L(ln1_g), L(ln1_b), alpha,
                                       w_router_t=w_router[j].T)
            dest, tile_e, n_active, n_slots = _moe_plan(eidx, n)
            xs = _dispatch(h1, dest, n_slots)
            y = _moe_gemm(xs, moe_w1_b, moe_w3_b, moe_w2_b, j * N_EXPERTS, tile_e, n_active)
            h = _moe_tail(h1, p_stack, i, gates.T, y, dest, *tail_params, alpha)
    return h.reshape(batch, seq_len, d)
```

```python
import collections
import functools

import numpy as np
import jax
import jax.numpy as jnp
from jax import lax
from jax.experimental import pallas as pl
from jax.experimental.pallas import tpu as pltpu

F32 = jnp.float32
BF16 = jnp.bfloat16

CONV_CH = 512
N_HEADS = 8
HEAD_DIM = 64
ATTN_CH = N_HEADS * HEAD_DIM
CONV_K = 31
GRID_W = 64
WIN_R = 8
WIN_C = 16
N_EXPERTS = 8
LN_EPS = 1e-5
MASK_NEG = -1e30

LANES = 128
SUBLANES = 8
MXU_N = 256
VMEM_LIMIT = 56 * 1024 * 1024

TM_PROJ = 512
TT_CONV = 512
CONV_HALO = 16
CONV_CHUNK = 32
CONV_CHUNK_LANES = 512
ROWS_PER_ATTN_STEP = 16
TM_MOE = 512
FC_MOE = 1792
TM_ROWDMA = 512
ROWDMA_UNROLL = 8


def _cparams(semantics):
    return pltpu.CompilerParams(dimension_semantics=semantics, vmem_limit_bytes=VMEM_LIMIT)


def _layer_norm(x, g, b):
    mu = jnp.mean(x, axis=-1, keepdims=True)
    xc = x - mu
    var = jnp.mean(xc * xc, axis=-1, keepdims=True)
    return xc * lax.rsqrt(var + LN_EPS) * g + b


class _Layered(collections.namedtuple("_Layered", ["stack", "layer"])):
    @property
    def shape(self):
        return self.stack.shape[1:]


def _param_spec(x):
    nd = len(x.shape)
    if isinstance(x, _Layered):
        layer = x.layer
        return pl.BlockSpec((None,) + tuple(x.shape), lambda *_: (layer,) + (0,) * nd,
                            pipeline_mode=pl.Buffered(1))
    return pl.BlockSpec(tuple(x.shape), lambda *_: (0,) * nd, pipeline_mode=pl.Buffered(1))


def _arr(x):
    return x.stack if isinstance(x, _Layered) else x


def _row_spec(tm, width):
    return pl.BlockSpec((tm, width), lambda i: (i, 0))


def _inproj_body(x_ref, lng_ref, lnb_ref, w_ref, b_ref, *out_refs, pre_ln):
    if pre_ln:
        h_ref, u_ref, q_ref, k_ref, v_ref = out_refs
    else:
        u_ref, q_ref, k_ref, v_ref = out_refs
    x = x_ref[...]
    if pre_ln:
        x = _layer_norm(x, lng_ref[...], lnb_ref[...])
        h_ref[...] = x
    xb = x.astype(BF16)

    def proj(lo, hi):
        return jnp.dot(xb, w_ref[:, lo:hi], preferred_element_type=F32) + b_ref[:, lo:hi]

    a = proj(0, CONV_CH)
    g = proj(CONV_CH, 2 * CONV_CH)
    u_ref[...] = a * jax.nn.sigmoid(g)
    base = 2 * CONV_CH
    q_ref[...] = proj(base, base + ATTN_CH).astype(BF16)
    k_ref[...] = proj(base + ATTN_CH, base + 2 * ATTN_CH).astype(BF16)
    v_ref[...] = proj(base + 2 * ATTN_CH, base + 3 * ATTN_CH).astype(BF16)


def _inproj(x, ln_g, ln_b, w, b, pre_ln):
    n, d = x.shape
    tm = TM_PROJ
    out_shape = [jax.ShapeDtypeStruct((n, CONV_CH), F32)] + [jax.ShapeDtypeStruct((n, ATTN_CH), BF16)] * 3
    out_specs = [_row_spec(tm, CONV_CH)] + [_row_spec(tm, ATTN_CH)] * 3
    if pre_ln:
        out_shape = [jax.ShapeDtypeStruct((n, d), F32)] + out_shape
        out_specs = [_row_spec(tm, d)] + out_specs
    params = [ln_g, ln_b, w, b]
    return pl.pallas_call(
        functools.partial(_inproj_body, pre_ln=pre_ln),
        grid=(n // tm,),
        in_specs=[_row_spec(tm, d)] + [_param_spec(a) for a in params],
        out_specs=out_specs,
        out_shape=out_shape,
        compiler_params=_cparams(("parallel",)),
        name="inproj_ln" if pre_ln else "inproj",
    )(x, *[_arr(a) for a in params])


def _conv_body(prev_ref, cur_ref, next_ref, w_ref, b_ref, lng_ref, lnb_ref, o_ref, ext_ref, sh_ref, acc_ref, *,
               tiles_per_seq):
    i = pl.program_id(0)
    tt = cur_ref.shape[0]
    pos = i % tiles_per_seq
    zeros = jnp.zeros((CONV_HALO, CONV_CH), F32)
    ext_ref[0:CONV_HALO, :] = jnp.where(pos > 0, prev_ref[...], zeros)
    ext_ref[CONV_HALO:CONV_HALO + tt, :] = cur_ref[...]
    ext_ref[CONV_HALO + tt:, :] = jnp.where(pos < tiles_per_seq - 1, next_ref[...], zeros)
    span = sh_ref.shape[1]
    for b in range(1, SUBLANES):
        sh_ref[b - 1] = ext_ref[b:b + span, :]
    shift = CONV_HALO - CONV_K // 2

    def chunk(c, carry):
        base = pl.multiple_of(c * CONV_CHUNK, CONV_CHUNK)
        groups = CONV_CHUNK // SUBLANES
        for lo in range(0, CONV_CH, CONV_CHUNK_LANES):
            cols = slice(lo, lo + CONV_CHUNK_LANES)
            acc = jnp.zeros((groups, SUBLANES, CONV_CHUNK_LANES), F32)
            for k in range(CONV_K):
                b, a = (k + shift) % SUBLANES, (k + shift) // SUBLANES
                rows = pl.ds(base + a * SUBLANES, CONV_CHUNK)
                tap = ext_ref[rows, cols] if b == 0 else sh_ref[b - 1, rows, cols]
                acc = acc + tap.reshape(groups, SUBLANES, CONV_CHUNK_LANES) * w_ref[k, :, cols][None]
            acc_ref[pl.ds(base, CONV_CHUNK), cols] = acc.reshape(CONV_CHUNK, CONV_CHUNK_LANES)
        return carry

    lax.fori_loop(0, tt // CONV_CHUNK, chunk, 0)
    y = _layer_norm(acc_ref[...] + b_ref[...], lng_ref[...], lnb_ref[...])
    o_ref[...] = (y * jax.nn.sigmoid(y)).astype(BF16)


def _conv_module(u, w_rep, b, ln_g, ln_b, seq_len):
    n = u.shape[0]
    tt = TT_CONV
    hb = tt // CONV_HALO
    n_halo_blocks = n // CONV_HALO
    params = [w_rep, b, ln_g, ln_b]
    return pl.pallas_call(
        functools.partial(_conv_body, tiles_per_seq=seq_len // tt),
        grid=(n // tt,),
        in_specs=[
            pl.BlockSpec((CONV_HALO, CONV_CH), lambda i: (jnp.maximum(i * hb - 1, 0), 0)),
            pl.BlockSpec((tt, CONV_CH), lambda i: (i, 0)),
            pl.BlockSpec((CONV_HALO, CONV_CH), lambda i: (jnp.minimum((i + 1) * hb, n_halo_blocks - 1), 0)),
        ] + [_param_spec(a) for a in params],
        out_specs=pl.BlockSpec((tt, CONV_CH), lambda i: (i, 0)),
        out_shape=jax.ShapeDtypeStruct((n, CONV_CH), BF16),
        scratch_shapes=[pltpu.VMEM((tt + 2 * CONV_HALO, CONV_CH), F32),
                        pltpu.VMEM((SUBLANES - 1, tt + 2 * CONV_HALO - SUBLANES, CONV_CH), F32),
                        pltpu.VMEM((tt, CONV_CH), F32)],
        compiler_params=_cparams(("parallel",)),
        name="conv_module",
    )(u, u, u, *[_arr(a) for a in params])


def _attn_bias_tables(rpb):
    c = np.arange(GRID_W)
    kc = np.arange(GRID_W)
    win0 = np.clip(c - WIN_C // 2, 0, GRID_W - WIN_C)
    col_mask = (kc[None, :] >= win0[:, None]) & (kc[None, :] < win0[:, None] + WIN_C)
    dc_idx = np.clip(kc[None, :] - c[:, None] + WIN_C - 1, 0, 2 * WIN_C - 2)
    sel_c = (dc_idx[:, :, None] == np.arange(2 * WIN_C - 1)[None, None, :]) & col_mask[:, :, None]
    delta = np.arange(WIN_R)
    dr_idx = np.arange(WIN_R)[None, :] - delta[:, None] + WIN_R - 1
    sel_r = dr_idx[:, :, None] == np.arange(2 * WIN_R - 1)[None, None, :]
    t = jnp.einsum("lhrd,xir,ckd->lhxcik", rpb, jnp.asarray(sel_r, F32), jnp.asarray(sel_c, F32),
                   precision=lax.Precision.HIGHEST)
    t = jnp.where(col_mask[None, None, None, :, None, :], t, MASK_NEG)
    n_layers = rpb.shape[0]
    t = t.reshape(n_layers, N_HEADS // 2, 2, WIN_R, GRID_W, WIN_R * GRID_W)
    t = jnp.transpose(t, (0, 1, 3, 2, 4, 5))
    return t.reshape(n_layers, N_HEADS // 2, WIN_R, 2 * GRID_W, WIN_R * GRID_W).astype(F32)


def _attn_body(q_ref, k_ref, v_ref, bias_ref, o_ref, s_ref, p_ref, l_ref, *, rows):
    g = pl.program_id(2)
    win_tokens = WIN_R * GRID_W
    lane = lax.broadcasted_iota(jnp.int32, (GRID_W, 2 * HEAD_DIM), 1)
    first = lane < HEAD_DIM
    scale = HEAD_DIM ** -0.5

    koffs = []
    for j in range(ROWS_PER_ATTN_STEP):
        r = g * ROWS_PER_ATTN_STEP + j
        r0 = jnp.clip(r - WIN_R // 2, 0, rows - WIN_R)
        koff = pl.multiple_of(r0 * GRID_W, GRID_W)
        koffs.append(koff)
        q2 = q_ref[j * GRID_W:(j + 1) * GRID_W, :] * scale
        zero = jnp.zeros_like(q2)
        qm = jnp.concatenate([jnp.where(first, q2, zero), jnp.where(first, zero, q2)], axis=0)
        ks = k_ref[pl.ds(koff, win_tokens), :]
        s = lax.dot_general(qm, ks, (((1,), (1,)), ((), ())), preferred_element_type=F32)
        s_ref[j] = s + bias_ref[r - r0]
    for j in range(ROWS_PER_ATTN_STEP):
        s = s_ref[j]
        m = jnp.max(s, axis=-1, keepdims=True)
        p = jnp.exp(s - m)
        l_ref[j] = jnp.broadcast_to(jnp.sum(p, axis=-1, keepdims=True), l_ref.shape[1:])
        p_ref[j] = p.astype(BF16)
    for j in range(ROWS_PER_ATTN_STEP):
        vs = v_ref[pl.ds(koffs[j], win_tokens), :]
        o = jnp.dot(p_ref[j], vs, preferred_element_type=F32) / l_ref[j]
        o2 = jnp.where(first, o[:GRID_W], o[GRID_W:])
        o_ref[j * GRID_W:(j + 1) * GRID_W, :] = o2.astype(BF16)


def _attention(q, k, v, bias_tabs, layer, batch, seq_len):
    n = q.shape[0]
    rows = seq_len // GRID_W
    tq = ROWS_PER_ATTN_STEP * GRID_W
    steps = seq_len // tq
    pair_w = 2 * HEAD_DIM
    return pl.pallas_call(
        functools.partial(_attn_body, rows=rows),
        grid=(N_HEADS // 2, batch, steps),
        in_specs=[
            pl.BlockSpec((tq, pair_w), lambda hp, b, g: (b * steps + g, hp)),
            pl.BlockSpec((seq_len, pair_w), lambda hp, b, g: (b, hp)),
            pl.BlockSpec((seq_len, pair_w), lambda hp, b, g: (b, hp)),
            pl.BlockSpec((None, None, WIN_R, 2 * GRID_W, WIN_R * GRID_W), lambda hp, b, g: (layer, hp, 0, 0, 0)),
        ],
        out_specs=pl.BlockSpec((tq, pair_w), lambda hp, b, g: (b * steps + g, hp)),
        out_shape=jax.ShapeDtypeStruct((n, ATTN_CH), BF16),
        scratch_shapes=[pltpu.VMEM((ROWS_PER_ATTN_STEP, 2 * GRID_W, WIN_R * GRID_W), F32),
                        pltpu.VMEM((ROWS_PER_ATTN_STEP, 2 * GRID_W, WIN_R * GRID_W), BF16),
                        pltpu.VMEM((ROWS_PER_ATTN_STEP, 2 * GRID_W, pair_w), F32)],
        compiler_params=_cparams(("parallel", "parallel", "arbitrary")),
        name="nbr_attention",
    )(q, k, v, bias_tabs)


def _mix_and_norm(h_ref, c_ref, a_ref, w_ref, b_ref, lng_ref, lnb_ref, alpha):
    mix = jnp.dot(c_ref[...], w_ref[0:CONV_CH, :], preferred_element_type=F32)
    mix = mix + jnp.dot(a_ref[...], w_ref[CONV_CH:, :], preferred_element_type=F32)
    mix = mix + b_ref[...]
    return _layer_norm(alpha * h_ref[...] + mix, lng_ref[...], lnb_ref[...])


def _outproj_router_body(h_ref, c_ref, a_ref, w_ref, b_ref, lng_ref, lnb_ref, wr_ref, h1_ref, eidx_ref, gate_ref, *,
                         alpha):
    h1 = _mix_and_norm(h_ref, c_ref, a_ref, w_ref, b_ref, lng_ref, lnb_ref, alpha)
    h1_ref[...] = h1
    logits = lax.dot_general(wr_ref[...], h1, (((1,), (1,)), ((), ())),
                             precision=lax.Precision.HIGHEST, preferred_element_type=F32)
    eid = lax.broadcasted_iota(jnp.int32, logits.shape, 0)
    m1 = jnp.max(logits, axis=0, keepdims=True)
    i1 = jnp.min(jnp.where(logits == m1, eid, N_EXPERTS), axis=0, keepdims=True)
    rest_l = jnp.where(eid == i1, -jnp.inf, logits)
    m2 = jnp.max(rest_l, axis=0, keepdims=True)
    i2 = jnp.min(jnp.where(rest_l == m2, eid, N_EXPERTS), axis=0, keepdims=True)
    e2 = jnp.exp(m2 - m1)
    den = 1.0 + e2
    eidx_ref[...] = jnp.concatenate([i1, i2], axis=0)
    gate_ref[...] = jnp.concatenate([1.0 / den, e2 / den], axis=0)


def _outproj_router(h, conv_out, attn_out, w, b, ln_g, ln_b, w_router_t, alpha):
    n, d = h.shape
    tm = TM_PROJ
    params = [w, b, ln_g, ln_b, w_router_t]
    return pl.pallas_call(
        functools.partial(_outproj_router_body, alpha=alpha),
        grid=(n // tm,),
        in_specs=[_row_spec(tm, d), _row_spec(tm, CONV_CH), _row_spec(tm, ATTN_CH)]
        + [_param_spec(a) for a in params],
        out_specs=[_row_spec(tm, d)] + [pl.BlockSpec((2, tm), lambda i: (0, i))] * 2,
        out_shape=[jax.ShapeDtypeStruct((n, d), F32), jax.ShapeDtypeStruct((2, n), jnp.int32),
                   jax.ShapeDtypeStruct((2, n), F32)],
        compiler_params=_cparams(("parallel",)),
        name="outproj_router",
    )(h, conv_out, attn_out, *[_arr(a) for a in params])


def _col_chunks(width, step):
    return [(lo, min(lo + step, width)) for lo in range(0, width, step)]


def _ple_and_norm(h1, f, p_ref, wg_ref, bg_ref, wp_ref, lng_ref, lnb_ref, alpha):
    hb = h1.astype(BF16)
    gate = jax.nn.sigmoid(jnp.dot(hb, wg_ref[...], preferred_element_type=F32) + bg_ref[...])
    e = gate * jnp.dot(p_ref[...].astype(BF16), wp_ref[...], preferred_element_type=F32)
    return _layer_norm(alpha * h1 + f + e, lng_ref[...], lnb_ref[...])


def _layer_rows_spec(tm, stack, layer):
    return pl.BlockSpec((None, tm, stack.shape[2]), lambda i: (layer, i, 0))


def _ffn_tail_body(h_ref, c_ref, a_ref, wo_ref, bo_ref, ln1g_ref, ln1b_ref, p_ref, w1_ref, w3_ref, w2_ref, wg_ref,
                   bg_ref, wp_ref, lng_ref, lnb_ref, o_ref, *, alpha):
    h1 = _mix_and_norm(h_ref, c_ref, a_ref, wo_ref, bo_ref, ln1g_ref, ln1b_ref, alpha)
    hb = h1.astype(BF16)
    f = jnp.zeros(h1.shape, F32)
    for lo, hi in _col_chunks(w1_ref.shape[1], 2 * MXU_N):
        a = jnp.dot(hb, w1_ref[:, lo:hi], preferred_element_type=F32)
        b = jnp.dot(hb, w3_ref[:, lo:hi], preferred_element_type=F32)
        mid = (a * jax.nn.sigmoid(a) * b).astype(BF16)
        f = f + jnp.dot(mid, w2_ref[lo:hi, :], preferred_element_type=F32)
    o_ref[...] = _ple_and_norm(h1, f, p_ref, wg_ref, bg_ref, wp_ref, lng_ref, lnb_ref, alpha)


def _ffn_tail(h, conv_out, attn_out, w_out, b_out, ln1_g, ln1_b, p_stack, layer, w1, w3, w2, wg, bg, wp, ln_g, ln_b,
              alpha):
    n, d = h.shape
    tm = TM_PROJ
    mix_params = [w_out, b_out, ln1_g, ln1_b]
    params = [w1, w3, w2, wg, bg, wp, ln_g, ln_b]
    return pl.pallas_call(
        functools.partial(_ffn_tail_body, alpha=alpha),
        grid=(n // tm,),
        in_specs=[_row_spec(tm, d), _row_spec(tm, CONV_CH), _row_spec(tm, ATTN_CH)]
        + [_param_spec(a) for a in mix_params] + [_layer_rows_spec(tm, p_stack, layer)]
        + [_param_spec(a) for a in params],
        out_specs=_row_spec(tm, d),
        out_shape=jax.ShapeDtypeStruct((n, d), F32),
        compiler_params=_cparams(("parallel",)),
        name="ffn_tail",
    )(h, conv_out, attn_out, *[_arr(a) for a in mix_params], p_stack, *[_arr(a) for a in params])


def _row_copy(src_ref, src_row, dst_ref, dst_row, sem):
    return pltpu.make_async_copy(src_ref.at[pl.ds(src_row, 1), :], dst_ref.at[pl.ds(dst_row, 1), :], sem)


def _slot_ids_copy(dest_hbm, idx_smem, sems, step):
    n_ids = idx_smem.shape[0] // 2
    start = pl.multiple_of(step * n_ids, n_ids)
    half = step % 2
    dst = idx_smem.at[pl.ds(pl.multiple_of(half * n_ids, n_ids), n_ids)]
    return pltpu.make_async_copy(dest_hbm.at[pl.ds(start, n_ids)], dst, sems.at[half])


def _for_token_groups(tm, fn):
    def group(o, carry):
        base = pl.multiple_of(o * ROWDMA_UNROLL, ROWDMA_UNROLL)
        for r in range(ROWDMA_UNROLL):
            fn(base + r)
        return carry

    lax.fori_loop(0, tm // ROWDMA_UNROLL, group, 0)


def _dispatch_body(pad_lo_ref, pad_hi_ref, dest_hbm, h_ref, xs_hbm, idx_smem, zrow_ref, sem_idx, sem_rows,
                   sem_pad):
    i = pl.program_id(0)
    n_steps = pl.num_programs(0)
    tm = h_ref.shape[0]
    half = i % 2

    @pl.when(i == 0)
    def _():
        _slot_ids_copy(dest_hbm, idx_smem, sem_idx, i).start()
        zrow_ref[...] = jnp.zeros(zrow_ref.shape, F32)

        def fill(r, carry):
            _row_copy(zrow_ref, 0, xs_hbm, r, sem_pad).start()
            return carry

        def fill_done(r, carry):
            _row_copy(zrow_ref, 0, xs_hbm, r, sem_pad).wait()
            return carry

        for e in range(pad_lo_ref.shape[0]):
            lax.fori_loop(pad_lo_ref[e], pad_hi_ref[e], fill, 0)
        for e in range(pad_lo_ref.shape[0]):
            lax.fori_loop(pad_lo_ref[e], pad_hi_ref[e], fill_done, 0)

    @pl.when(i + 1 < n_steps)
    def _():
        _slot_ids_copy(dest_hbm, idx_smem, sem_idx, i + 1).start()

    _slot_ids_copy(dest_hbm, idx_smem, sem_idx, i).wait()

    ids0 = half * (2 * tm)

    def issue(t):
        _row_copy(h_ref, t, xs_hbm, idx_smem[ids0 + 2 * t], sem_rows).start()
        _row_copy(h_ref, t, xs_hbm, idx_smem[ids0 + 2 * t + 1], sem_rows).start()

    def drain(t):
        _row_copy(h_ref, t, xs_hbm, 0, sem_rows).wait()
        _row_copy(h_ref, t, xs_hbm, 0, sem_rows).wait()

    _for_token_groups(tm, issue)
    _for_token_groups(tm, drain)


def _dispatch(h1, dest_flat, pad_lo, pad_hi, n_slots):
    n, d = h1.shape
    tm = TM_ROWDMA
    grid_spec = pltpu.PrefetchScalarGridSpec(
        num_scalar_prefetch=2,
        grid=(n // tm,),
        in_specs=[pl.BlockSpec(memory_space=pl.ANY), pl.BlockSpec((tm, d), lambda i, lo, hi: (i, 0))],
        out_specs=pl.BlockSpec(memory_space=pl.ANY),
        scratch_shapes=[pltpu.SMEM((4 * tm,), jnp.int32), pltpu.VMEM((SUBLANES, d), F32),
                        pltpu.SemaphoreType.DMA((2,)), pltpu.SemaphoreType.DMA(()), pltpu.SemaphoreType.DMA(())],
    )
    return pl.pallas_call(
        _dispatch_body,
        grid_spec=grid_spec,
        out_shape=jax.ShapeDtypeStruct((n_slots, d), F32),
        compiler_params=_cparams(("arbitrary",)),
        name="moe_dispatch",
    )(pad_lo, pad_hi, dest_flat, h1)


def _moe_gemm_body(tile_e_ref, n_active_ref, x_ref, w1_ref, w3_ref, w2_ref, o_ref):
    del tile_e_ref
    i = pl.program_id(0)
    j = pl.program_id(1)

    @pl.when(j == 0)
    def _():
        o_ref[...] = jnp.zeros(o_ref.shape, F32)

    @pl.when(i < n_active_ref[0])
    def _():
        xb = x_ref[...].astype(BF16)
        a = jnp.dot(xb, w1_ref[...], preferred_element_type=F32)
        b = jnp.dot(xb, w3_ref[...], preferred_element_type=F32)
        mid = (a * jax.nn.sigmoid(a) * b).astype(BF16)
        o_ref[...] += jnp.dot(mid, w2_ref[...], preferred_element_type=F32)


def _moe_gemm(xs, w1, w3, w2, first_expert, tile_e, n_active):
    n_slots, d = xs.shape
    f = w1.shape[2]
    tm, fc = TM_MOE, FC_MOE
    grid_spec = pltpu.PrefetchScalarGridSpec(
        num_scalar_prefetch=2,
        grid=(n_slots // tm, f // fc),
        in_specs=[
            pl.BlockSpec((tm, d), lambda i, j, te, na: (i, 0)),
            pl.BlockSpec((None, d, fc), lambda i, j, te, na: (first_expert + te[i], 0, j)),
            pl.BlockSpec((None, d, fc), lambda i, j, te, na: (first_expert + te[i], 0, j)),
            pl.BlockSpec((None, fc, d), lambda i, j, te, na: (first_expert + te[i], j, 0)),
        ],
        out_specs=pl.BlockSpec((tm, d), lambda i, j, te, na: (i, 0)),
    )
    return pl.pallas_call(
        _moe_gemm_body,
        grid_spec=grid_spec,
        out_shape=jax.ShapeDtypeStruct((n_slots, d), F32),
        compiler_params=_cparams(("parallel", "arbitrary")),
        name="moe_gemm",
    )(tile_e, n_active, xs, w1, w3, w2)


def _moe_tail_body(dest_hbm, h_ref, p_ref, g_ref, y_hbm, wg_ref, bg_ref, wp_ref, lng_ref, lnb_ref, o_ref,
                   idx_smem, buf, sem_idx, sem_rows, *, alpha):
    i = pl.program_id(0)
    n_steps = pl.num_programs(0)
    tm = h_ref.shape[0]
    half = i % 2

    def issue_gathers(step):
        hs = step % 2
        ids0 = hs * (2 * tm)

        def issue(t):
            _row_copy(y_hbm, idx_smem[ids0 + 2 * t], buf.at[hs, 0], t, sem_rows.at[hs]).start()
            _row_copy(y_hbm, idx_smem[ids0 + 2 * t + 1], buf.at[hs, 1], t, sem_rows.at[hs]).start()

        _for_token_groups(tm, issue)

    @pl.when(i == 0)
    def _():
        cp = _slot_ids_copy(dest_hbm, idx_smem, sem_idx, i)
        cp.start()
        cp.wait()
        issue_gathers(i)

    @pl.when(i + 1 < n_steps)
    def _():
        cp = _slot_ids_copy(dest_hbm, idx_smem, sem_idx, i + 1)
        cp.start()
        cp.wait()
        issue_gathers(i + 1)

    def drain(t):
        _row_copy(y_hbm, 0, buf.at[half, 0], t, sem_rows.at[half]).wait()
        _row_copy(y_hbm, 0, buf.at[half, 1], t, sem_rows.at[half]).wait()

    _for_token_groups(tm, drain)
    g = g_ref[...]
    f = buf[half, 0] * g[:, 0:1] + buf[half, 1] * g[:, 1:2]
    o_ref[...] = _ple_and_norm(h_ref[...], f, p_ref, wg_ref, bg_ref, wp_ref, lng_ref, lnb_ref, alpha)


def _moe_tail(h1, p_stack, layer, gates_t, y, dest_flat, wg, bg, wp, ln_g, ln_b, alpha):
    n, d = h1.shape
    tm = TM_ROWDMA
    params = [wg, bg, wp, ln_g, ln_b]
    return pl.pallas_call(
        functools.partial(_moe_tail_body, alpha=alpha),
        grid=(n // tm,),
        in_specs=[pl.BlockSpec(memory_space=pl.ANY), _row_spec(tm, d), _layer_rows_spec(tm, p_stack, layer),
                  _row_spec(tm, 2), pl.BlockSpec(memory_space=pl.ANY)] + [_param_spec(a) for a in params],
        out_specs=_row_spec(tm, d),
        out_shape=jax.ShapeDtypeStruct((n, d), F32),
        scratch_shapes=[pltpu.SMEM((4 * tm,), jnp.int32), pltpu.VMEM((2, 2, tm, d), F32),
                        pltpu.SemaphoreType.DMA((2,)), pltpu.SemaphoreType.DMA((2,))],
        compiler_params=_cparams(("arbitrary",)),
        name="moe_tail",
    )(dest_flat, h1, p_stack, gates_t, y, *[_arr(a) for a in params])


def _moe_plan(eidx, n_tokens):
    tm = TM_MOE
    e_flat = eidx.T.reshape(-1)
    onehot = (e_flat[:, None] == jnp.arange(N_EXPERTS, dtype=jnp.int32)[None, :]).astype(jnp.int32)
    csum = jnp.cumsum(onehot, axis=0)
    rank = jnp.sum(csum * onehot, axis=1) - 1
    counts = csum[-1]
    padded = (counts + tm - 1) // tm * tm
    pends = jnp.cumsum(padded)
    pstarts = pends - padded
    dest = (jnp.sum(pstarts[None, :] * onehot, axis=1) + rank).astype(jnp.int32)
    n_tiles = (2 * n_tokens) // tm + N_EXPERTS
    tile_start = jnp.arange(n_tiles, dtype=jnp.int32) * tm
    tile_e = jnp.minimum(jnp.sum((tile_start[:, None] >= pends[None, :]).astype(jnp.int32), axis=1),
                         N_EXPERTS - 1).astype(jnp.int32)
    n_active = (pends[-1] // tm).astype(jnp.int32).reshape(1)
    n_slots = n_tiles * tm
    pad_lo = jnp.concatenate([pstarts + counts, pends[-1:]]).astype(jnp.int32)
    pad_hi = jnp.concatenate([pends, jnp.full((1,), n_slots, pends.dtype)]).astype(jnp.int32)
    return dest, tile_e, n_active, pad_lo, pad_hi, n_slots


def kernel(x, p, ln_in_g, ln_in_b, w_in, b_in, conv_w, conv_b, conv_ln_g, conv_ln_b, rpb, w_out, b_out,
           ln1_g, ln1_b, ffn_w1, ffn_w3, ffn_w2, w_router, moe_w1, moe_w3, moe_w2, ple_w, ple_gate_w,
           ple_gate_b, ln2_g, ln2_b):
    batch, seq_len, d = x.shape
    depth = w_in.shape[0]
    n = batch * seq_len
    alpha = (2.0 * depth) ** 0.25

    rows = lambda v: v.reshape(v.shape[0], 1, v.shape[-1])
    ln_in_g, ln_in_b = ln_in_g.reshape(1, 1, d), ln_in_b.reshape(1, 1, d)
    w_in_b, w_out_b = w_in.astype(BF16), w_out.astype(BF16)
    ffn_w1_b, ffn_w3_b, ffn_w2_b = ffn_w1.astype(BF16), ffn_w3.astype(BF16), ffn_w2.astype(BF16)
    wg_b, wp_b = ple_gate_w.astype(BF16), ple_w.astype(BF16)
    moe_shape = lambda w: w.reshape((-1,) + w.shape[2:])
    moe_w1_b, moe_w3_b, moe_w2_b = (moe_shape(w).astype(BF16) for w in (moe_w1, moe_w3, moe_w2))
    b_in, conv_b, conv_ln_g, conv_ln_b = rows(b_in), rows(conv_b), rows(conv_ln_g), rows(conv_ln_b)
    b_out, ln1_g, ln1_b, ple_gate_b, ln2_g, ln2_b = (rows(v) for v in (b_out, ln1_g, ln1_b, ple_gate_b, ln2_g, ln2_b))
    conv_w_rep = jnp.broadcast_to(conv_w[:, :, None, :], conv_w.shape[:2] + (SUBLANES, conv_w.shape[2]))
    bias_tabs = _attn_bias_tables(rpb)
    p_stack = p.reshape(depth, n, p.shape[-1])

    h = x.reshape(n, d)
    for i in range(depth):
        L = lambda stack, layer=i: _Layered(stack, layer)
        if i == 0:
            h, u, q, k, v = _inproj(h, L(ln_in_g, 0), L(ln_in_b, 0), L(w_in_b), L(b_in), pre_ln=True)
        else:
            u, q, k, v = _inproj(h, L(ln_in_g, 0), L(ln_in_b, 0), L(w_in_b), L(b_in), pre_ln=False)
        conv_out = _conv_module(u, L(conv_w_rep), L(conv_b), L(conv_ln_g), L(conv_ln_b), seq_len)
        attn_out = _attention(q, k, v, bias_tabs, i, batch, seq_len)
        j = i // 2
        tail_params = (L(wg_b), L(ple_gate_b), L(wp_b), L(ln2_g), L(ln2_b))
        mix_params = (L(w_out_b), L(b_out), L(ln1_g), L(ln1_b))
        if i % 2 == 0:
            h = _ffn_tail(h, conv_out, attn_out, *mix_params, p_stack, i, L(ffn_w1_b, j), L(ffn_w3_b, j),
                          L(ffn_w2_b, j), *tail_params, alpha)
        else:
            h1, eidx, gates = _outproj_router(h, conv_out, attn_out, *mix_params, w_router[j].T, alpha)
            dest, tile_e, n_active, pad_lo, pad_hi, n_slots = _moe_plan(eidx, n)
            xs = _dispatch(h1, dest, pad_lo, pad_hi, n_slots)
            y = _moe_gemm(xs, moe_w1_b, moe_w3_b, moe_w2_b, j * N_EXPERTS, tile_e, n_active)
            h = _moe_tail(h1, p_stack, i, gates.T, y, dest, *tail_params, alpha)
    return h.reshape(batch, seq_len, d)
```

```python
import collections
import functools

import numpy as np
import jax
import jax.numpy as jnp
from jax import lax
from jax.experimental import pallas as pl
from jax.experimental.pallas import tpu as pltpu

F32 = jnp.float32
BF16 = jnp.bfloat16

CONV_CH = 512
N_HEADS = 8
HEAD_DIM = 64
ATTN_CH = N_HEADS * HEAD_DIM
CONV_K = 31
GRID_W = 64
WIN_R = 8
WIN_C = 16
N_EXPERTS = 8
LN_EPS = 1e-5
MASK_NEG = -1e30

LANES = 128
SUBLANES = 8
MXU_N = 256
VMEM_LIMIT = 56 * 1024 * 1024

TM_PROJ = 512
TM_INPROJ = 1024
TT_CONV = 1024
CONV_HALO = 16
CONV_CHUNK = 32
CONV_CHUNK_LANES = 512
ROWS_PER_ATTN_STEP = 32
TM_MOE = 512
FC_MOE = 1792
TM_ROWDMA = 512
ROWDMA_UNROLL = 8


def _cparams(semantics):
    return pltpu.CompilerParams(dimension_semantics=semantics, vmem_limit_bytes=VMEM_LIMIT)


def _layer_norm(x, g, b):
    mu = jnp.mean(x, axis=-1, keepdims=True)
    xc = x - mu
    var = jnp.mean(xc * xc, axis=-1, keepdims=True)
    return xc * lax.rsqrt(var + LN_EPS) * g + b


class _Layered(collections.namedtuple("_Layered", ["stack", "layer"])):
    @property
    def shape(self):
        return self.stack.shape[1:]


def _param_spec(x):
    nd = len(x.shape)
    if isinstance(x, _Layered):
        layer = x.layer
        return pl.BlockSpec((None,) + tuple(x.shape), lambda *_: (layer,) + (0,) * nd,
                            pipeline_mode=pl.Buffered(1))
    return pl.BlockSpec(tuple(x.shape), lambda *_: (0,) * nd, pipeline_mode=pl.Buffered(1))


def _arr(x):
    return x.stack if isinstance(x, _Layered) else x


def _row_spec(tm, width):
    return pl.BlockSpec((tm, width), lambda i: (i, 0))


def _inproj_body(x_ref, lng_ref, lnb_ref, w_ref, b_ref, *out_refs, pre_ln):
    if pre_ln:
        h_ref, u_ref, q_ref, k_ref, v_ref = out_refs
    else:
        u_ref, q_ref, k_ref, v_ref = out_refs
    x = x_ref[...]
    if pre_ln:
        x = _layer_norm(x, lng_ref[...], lnb_ref[...])
        h_ref[...] = x
    xb = x.astype(BF16)

    def proj(lo, hi):
        return jnp.dot(xb, w_ref[:, lo:hi], preferred_element_type=F32) + b_ref[:, lo:hi]

    a = proj(0, CONV_CH)
    g = proj(CONV_CH, 2 * CONV_CH)
    u_ref[...] = a * jax.nn.sigmoid(g)
    base = 2 * CONV_CH
    q_ref[...] = proj(base, base + ATTN_CH).astype(BF16)
    k_ref[...] = proj(base + ATTN_CH, base + 2 * ATTN_CH).astype(BF16)
    v_ref[...] = proj(base + 2 * ATTN_CH, base + 3 * ATTN_CH).astype(BF16)


def _inproj(x, ln_g, ln_b, w, b, pre_ln):
    n, d = x.shape
    tm = TM_INPROJ
    out_shape = [jax.ShapeDtypeStruct((n, CONV_CH), F32)] + [jax.ShapeDtypeStruct((n, ATTN_CH), BF16)] * 3
    out_specs = [_row_spec(tm, CONV_CH)] + [_row_spec(tm, ATTN_CH)] * 3
    if pre_ln:
        out_shape = [jax.ShapeDtypeStruct((n, d), F32)] + out_shape
        out_specs = [_row_spec(tm, d)] + out_specs
    params = [ln_g, ln_b, w, b]
    return pl.pallas_call(
        functools.partial(_inproj_body, pre_ln=pre_ln),
        grid=(n // tm,),
        in_specs=[_row_spec(tm, d)] + [_param_spec(a) for a in params],
        out_specs=out_specs,
        out_shape=out_shape,
        compiler_params=_cparams(("parallel",)),
        name="inproj_ln" if pre_ln else "inproj",
    )(x, *[_arr(a) for a in params])


def _conv_body(prev_ref, cur_ref, next_ref, w_ref, b_ref, lng_ref, lnb_ref, o_ref, ext_ref, sh_ref, acc_ref, *,
               tiles_per_seq):
    i = pl.program_id(0)
    tt = cur_ref.shape[0]
    pos = i % tiles_per_seq
    zeros = jnp.zeros((CONV_HALO, CONV_CH), F32)
    ext_ref[0:CONV_HALO, :] = jnp.where(pos > 0, prev_ref[...], zeros)
    ext_ref[CONV_HALO:CONV_HALO + tt, :] = cur_ref[...]
    ext_ref[CONV_HALO + tt:, :] = jnp.where(pos < tiles_per_seq - 1, next_ref[...], zeros)
    span = sh_ref.shape[1]
    for b in range(1, SUBLANES):
        sh_ref[b - 1] = ext_ref[b:b + span, :]
    shift = CONV_HALO - CONV_K // 2

    def chunk(c, carry):
        base = pl.multiple_of(c * CONV_CHUNK, CONV_CHUNK)
        groups = CONV_CHUNK // SUBLANES
        for lo in range(0, CONV_CH, CONV_CHUNK_LANES):
            cols = slice(lo, lo + CONV_CHUNK_LANES)
            acc = jnp.zeros((groups, SUBLANES, CONV_CHUNK_LANES), F32)
            for k in range(CONV_K):
                b, a = (k + shift) % SUBLANES, (k + shift) // SUBLANES
                rows = pl.ds(base + a * SUBLANES, CONV_CHUNK)
                tap = ext_ref[rows, cols] if b == 0 else sh_ref[b - 1, rows, cols]
                acc = acc + tap.reshape(groups, SUBLANES, CONV_CHUNK_LANES) * w_ref[k, :, cols][None]
            acc_ref[pl.ds(base, CONV_CHUNK), cols] = acc.reshape(CONV_CHUNK, CONV_CHUNK_LANES)
        return carry

    lax.fori_loop(0, tt // CONV_CHUNK, chunk, 0)
    y = _layer_norm(acc_ref[...] + b_ref[...], lng_ref[...], lnb_ref[...])
    o_ref[...] = (y * jax.nn.sigmoid(y)).astype(BF16)


def _conv_module(u, w_rep, b, ln_g, ln_b, seq_len):
    n = u.shape[0]
    tt = TT_CONV
    hb = tt // CONV_HALO
    n_halo_blocks = n // CONV_HALO
    params = [w_rep, b, ln_g, ln_b]
    return pl.pallas_call(
        functools.partial(_conv_body, tiles_per_seq=seq_len // tt),
        grid=(n // tt,),
        in_specs=[
            pl.BlockSpec((CONV_HALO, CONV_CH), lambda i: (jnp.maximum(i * hb - 1, 0), 0)),
            pl.BlockSpec((tt, CONV_CH), lambda i: (i, 0)),
            pl.BlockSpec((CONV_HALO, CONV_CH), lambda i: (jnp.minimum((i + 1) * hb, n_halo_blocks - 1), 0)),
        ] + [_param_spec(a) for a in params],
        out_specs=pl.BlockSpec((tt, CONV_CH), lambda i: (i, 0)),
        out_shape=jax.ShapeDtypeStruct((n, CONV_CH), BF16),
        scratch_shapes=[pltpu.VMEM((tt + 2 * CONV_HALO, CONV_CH), F32),
                        pltpu.VMEM((SUBLANES - 1, tt + 2 * CONV_HALO - SUBLANES, CONV_CH), F32),
                        pltpu.VMEM((tt, CONV_CH), F32)],
        compiler_params=_cparams(("parallel",)),
        name="conv_module",
    )(u, u, u, *[_arr(a) for a in params])


def _attn_bias_tables(rpb):
    c = np.arange(GRID_W)
    kc = np.arange(GRID_W)
    win0 = np.clip(c - WIN_C // 2, 0, GRID_W - WIN_C)
    col_mask = (kc[None, :] >= win0[:, None]) & (kc[None, :] < win0[:, None] + WIN_C)
    dc_idx = np.clip(kc[None, :] - c[:, None] + WIN_C - 1, 0, 2 * WIN_C - 2)
    sel_c = (dc_idx[:, :, None] == np.arange(2 * WIN_C - 1)[None, None, :]) & col_mask[:, :, None]
    delta = np.arange(WIN_R)
    dr_idx = np.arange(WIN_R)[None, :] - delta[:, None] + WIN_R - 1
    sel_r = dr_idx[:, :, None] == np.arange(2 * WIN_R - 1)[None, None, :]
    n_layers = rpb.shape[0]
    rpb_pairs = rpb.reshape(n_layers, N_HEADS // 2, 2, 2 * WIN_R - 1, 2 * WIN_C - 1)
    t = jnp.einsum("lptrd,xir,ckd->lpxtcik", rpb_pairs, jnp.asarray(sel_r, F32), jnp.asarray(sel_c, F32),
                   precision=lax.Precision.HIGHEST)
    t = jnp.where(col_mask[None, None, None, None, :, None, :], t, MASK_NEG)
    return t.reshape(n_layers, N_HEADS // 2, WIN_R, 2 * GRID_W, WIN_R * GRID_W).astype(F32)


def _attn_body(q_ref, k_ref, v_ref, bias_ref, o_ref, s_ref, p_ref, l_ref, *, rows):
    g = pl.program_id(2)
    win_tokens = WIN_R * GRID_W
    lane = lax.broadcasted_iota(jnp.int32, (GRID_W, 2 * HEAD_DIM), 1)
    first = lane < HEAD_DIM
    scale = HEAD_DIM ** -0.5

    koffs = []
    for j in range(ROWS_PER_ATTN_STEP):
        r = g * ROWS_PER_ATTN_STEP + j
        r0 = jnp.clip(r - WIN_R // 2, 0, rows - WIN_R)
        koff = pl.multiple_of(r0 * GRID_W, GRID_W)
        koffs.append(koff)
        q2 = q_ref[j * GRID_W:(j + 1) * GRID_W, :] * scale
        zero = jnp.zeros_like(q2)
        qm = jnp.concatenate([jnp.where(first, q2, zero), jnp.where(first, zero, q2)], axis=0)
        ks = k_ref[pl.ds(koff, win_tokens), :]
        s = lax.dot_general(qm, ks, (((1,), (1,)), ((), ())), preferred_element_type=F32)
        s_ref[j] = s + bias_ref[r - r0]
    for j in range(ROWS_PER_ATTN_STEP):
        s = s_ref[j]
        m = jnp.max(s, axis=-1, keepdims=True)
        p = jnp.exp(s - m)
        l_ref[j] = jnp.broadcast_to(jnp.sum(p, axis=-1, keepdims=True), l_ref.shape[1:])
        p_ref[j] = p.astype(BF16)
    for j in range(ROWS_PER_ATTN_STEP):
        vs = v_ref[pl.ds(koffs[j], win_tokens), :]
        o = jnp.dot(p_ref[j], vs, preferred_element_type=F32) / l_ref[j]
        o2 = jnp.where(first, o[:GRID_W], o[GRID_W:])
        o_ref[j * GRID_W:(j + 1) * GRID_W, :] = o2.astype(BF16)


def _attention(q, k, v, bias_tabs, layer, batch, seq_len):
    n = q.shape[0]
    rows = seq_len // GRID_W
    tq = ROWS_PER_ATTN_STEP * GRID_W
    steps = seq_len // tq
    pair_w = 2 * HEAD_DIM
    return pl.pallas_call(
        functools.partial(_attn_body, rows=rows),
        grid=(N_HEADS // 2, batch, steps),
        in_specs=[
            pl.BlockSpec((tq, pair_w), lambda hp, b, g: (b * steps + g, hp)),
            pl.BlockSpec((seq_len, pair_w), lambda hp, b, g: (b, hp)),
            pl.BlockSpec((seq_len, pair_w), lambda hp, b, g: (b, hp)),
            pl.BlockSpec((None, None, WIN_R, 2 * GRID_W, WIN_R * GRID_W), lambda hp, b, g: (layer, hp, 0, 0, 0)),
        ],
        out_specs=pl.BlockSpec((tq, pair_w), lambda hp, b, g: (b * steps + g, hp)),
        out_shape=jax.ShapeDtypeStruct((n, ATTN_CH), BF16),
        scratch_shapes=[pltpu.VMEM((ROWS_PER_ATTN_STEP, 2 * GRID_W, WIN_R * GRID_W), F32),
                        pltpu.VMEM((ROWS_PER_ATTN_STEP, 2 * GRID_W, WIN_R * GRID_W), BF16),
                        pltpu.VMEM((ROWS_PER_ATTN_STEP, 2 * GRID_W, pair_w), F32)],
        compiler_params=_cparams(("parallel", "parallel", "arbitrary")),
        name="nbr_attention",
    )(q, k, v, bias_tabs)


def _mix_and_norm(h_ref, c_ref, a_ref, w_ref, b_ref, lng_ref, lnb_ref, alpha):
    mix = jnp.dot(c_ref[...], w_ref[0:CONV_CH, :], preferred_element_type=F32)
    mix = mix + jnp.dot(a_ref[...], w_ref[CONV_CH:, :], preferred_element_type=F32)
    mix = mix + b_ref[...]
    return _layer_norm(alpha * h_ref[...] + mix, lng_ref[...], lnb_ref[...])


def _outproj_router_body(h_ref, c_ref, a_ref, w_ref, b_ref, lng_ref, lnb_ref, wr_ref, h1_ref, eidx_ref, gate_ref, *,
                         alpha):
    h1 = _mix_and_norm(h_ref, c_ref, a_ref, w_ref, b_ref, lng_ref, lnb_ref, alpha)
    h1_ref[...] = h1
    logits = lax.dot_general(wr_ref[...], h1, (((1,), (1,)), ((), ())),
                             precision=lax.Precision.HIGHEST, preferred_element_type=F32)
    eid = lax.broadcasted_iota(jnp.int32, logits.shape, 0)
    m1 = jnp.max(logits, axis=0, keepdims=True)
    i1 = jnp.min(jnp.where(logits == m1, eid, N_EXPERTS), axis=0, keepdims=True)
    rest_l = jnp.where(eid == i1, -jnp.inf, logits)
    m2 = jnp.max(rest_l, axis=0, keepdims=True)
    i2 = jnp.min(jnp.where(rest_l == m2, eid, N_EXPERTS), axis=0, keepdims=True)
    e2 = jnp.exp(m2 - m1)
    den = 1.0 + e2
    eidx_ref[...] = jnp.concatenate([i1, i2], axis=0)
    gate_ref[...] = jnp.concatenate([1.0 / den, e2 / den], axis=0)


def _outproj_router(h, conv_out, attn_out, w, b, ln_g, ln_b, w_router_t, alpha):
    n, d = h.shape
    tm = TM_PROJ
    params = [w, b, ln_g, ln_b, w_router_t]
    return pl.pallas_call(
        functools.partial(_outproj_router_body, alpha=alpha),
        grid=(n // tm,),
        in_specs=[_row_spec(tm, d), _row_spec(tm, CONV_CH), _row_spec(tm, ATTN_CH)]
        + [_param_spec(a) for a in params],
        out_specs=[_row_spec(tm, d)] + [pl.BlockSpec((2, tm), lambda i: (0, i))] * 2,
        out_shape=[jax.ShapeDtypeStruct((n, d), F32), jax.ShapeDtypeStruct((2, n), jnp.int32),
                   jax.ShapeDtypeStruct((2, n), F32)],
        compiler_params=_cparams(("parallel",)),
        name="outproj_router",
    )(h, conv_out, attn_out, *[_arr(a) for a in params])


def _col_chunks(width, step):
    return [(lo, min(lo + step, width)) for lo in range(0, width, step)]


def _ple_and_norm(h1, f, p_ref, wg_ref, bg_ref, wp_ref, lng_ref, lnb_ref, alpha):
    hb = h1.astype(BF16)
    gate = jax.nn.sigmoid(jnp.dot(hb, wg_ref[...], preferred_element_type=F32) + bg_ref[...])
    e = gate * jnp.dot(p_ref[...].astype(BF16), wp_ref[...], preferred_element_type=F32)
    return _layer_norm(alpha * h1 + f + e, lng_ref[...], lnb_ref[...])


def _layer_rows_spec(tm, stack, layer):
    return pl.BlockSpec((None, tm, stack.shape[2]), lambda i: (layer, i, 0))


def _ffn_tail_body(h_ref, c_ref, a_ref, wo_ref, bo_ref, ln1g_ref, ln1b_ref, p_ref, w1_ref, w3_ref, w2_ref, wg_ref,
                   bg_ref, wp_ref, lng_ref, lnb_ref, o_ref, *, alpha):
    h1 = _mix_and_norm(h_ref, c_ref, a_ref, wo_ref, bo_ref, ln1g_ref, ln1b_ref, alpha)
    hb = h1.astype(BF16)
    f = jnp.zeros(h1.shape, F32)
    for lo, hi in _col_chunks(w1_ref.shape[1], 2 * MXU_N):
        a = jnp.dot(hb, w1_ref[:, lo:hi], preferred_element_type=F32)
        b = jnp.dot(hb, w3_ref[:, lo:hi], preferred_element_type=F32)
        mid = (a * jax.nn.sigmoid(a) * b).astype(BF16)
        f = f + jnp.dot(mid, w2_ref[lo:hi, :], preferred_element_type=F32)
    o_ref[...] = _ple_and_norm(h1, f, p_ref, wg_ref, bg_ref, wp_ref, lng_ref, lnb_ref, alpha)


def _ffn_tail(h, conv_out, attn_out, w_out, b_out, ln1_g, ln1_b, p_stack, layer, w1, w3, w2, wg, bg, wp, ln_g, ln_b,
              alpha):
    n, d = h.shape
    tm = TM_PROJ
    mix_params = [w_out, b_out, ln1_g, ln1_b]
    params = [w1, w3, w2, wg, bg, wp, ln_g, ln_b]
    return pl.pallas_call(
        functools.partial(_ffn_tail_body, alpha=alpha),
        grid=(n // tm,),
        in_specs=[_row_spec(tm, d), _row_spec(tm, CONV_CH), _row_spec(tm, ATTN_CH)]
        + [_param_spec(a) for a in mix_params] + [_layer_rows_spec(tm, p_stack, layer)]
        + [_param_spec(a) for a in params],
        out_specs=_row_spec(tm, d),
        out_shape=jax.ShapeDtypeStruct((n, d), F32),
        compiler_params=_cparams(("parallel",)),
        name="ffn_tail",
    )(h, conv_out, attn_out, *[_arr(a) for a in mix_params], p_stack, *[_arr(a) for a in params])


def _row_copy(src_ref, src_row, dst_ref, dst_row, sem):
    return pltpu.make_async_copy(src_ref.at[pl.ds(src_row, 1), :], dst_ref.at[pl.ds(dst_row, 1), :], sem)


def _slot_ids_copy(dest_hbm, idx_smem, sems, step):
    n_ids = idx_smem.shape[0] // 2
    start = pl.multiple_of(step * n_ids, n_ids)
    half = step % 2
    dst = idx_smem.at[pl.ds(pl.multiple_of(half * n_ids, n_ids), n_ids)]
    return pltpu.make_async_copy(dest_hbm.at[pl.ds(start, n_ids)], dst, sems.at[half])


def _for_token_groups(tm, fn):
    def group(o, carry):
        base = pl.multiple_of(o * ROWDMA_UNROLL, ROWDMA_UNROLL)
        for r in range(ROWDMA_UNROLL):
            fn(base + r)
        return carry

    lax.fori_loop(0, tm // ROWDMA_UNROLL, group, 0)


def _zero_rows(zrow_ref, xs_hbm, lo, hi, sem, start):
    blk = zrow_ref.shape[0]
    head_end = jnp.minimum((lo + blk - 1) // blk * blk, hi)
    tail_start = jnp.maximum(head_end, hi // blk * blk)

    def single(r, carry):
        cp = _row_copy(zrow_ref, 0, xs_hbm, r, sem)
        cp.start() if start else cp.wait()
        return carry

    def block(k, carry):
        r = pl.multiple_of(head_end + k * blk, blk)
        cp = pltpu.make_async_copy(zrow_ref, xs_hbm.at[pl.ds(r, blk), :], sem)
        cp.start() if start else cp.wait()
        return carry

    lax.fori_loop(lo, head_end, single, 0)
    lax.fori_loop(0, (tail_start - head_end) // blk, block, 0)
    lax.fori_loop(tail_start, hi, single, 0)


def _dispatch_body(pad_lo_ref, pad_hi_ref, dest_hbm, h_ref, xs_hbm, idx_smem, zrow_ref, sem_idx, sem_rows,
                   sem_pad):
    i = pl.program_id(0)
    n_steps = pl.num_programs(0)
    tm = h_ref.shape[0]
    half = i % 2

    @pl.when(i == 0)
    def _():
        _slot_ids_copy(dest_hbm, idx_smem, sem_idx, i).start()
        zrow_ref[...] = jnp.zeros(zrow_ref.shape, F32)
        for start in (True, False):
            for e in range(pad_lo_ref.shape[0]):
                _zero_rows(zrow_ref, xs_hbm, pad_lo_ref[e], pad_hi_ref[e], sem_pad, start)

    @pl.when(i + 1 < n_steps)
    def _():
        _slot_ids_copy(dest_hbm, idx_smem, sem_idx, i + 1).start()

    _slot_ids_copy(dest_hbm, idx_smem, sem_idx, i).wait()

    ids0 = half * (2 * tm)

    def issue(t):
        _row_copy(h_ref, t, xs_hbm, idx_smem[ids0 + 2 * t], sem_rows).start()
        _row_copy(h_ref, t, xs_hbm, idx_smem[ids0 + 2 * t + 1], sem_rows).start()

    def drain(t):
        _row_copy(h_ref, t, xs_hbm, 0, sem_rows).wait()
        _row_copy(h_ref, t, xs_hbm, 0, sem_rows).wait()

    _for_token_groups(tm, issue)
    _for_token_groups(tm, drain)


def _dispatch(h1, dest_flat, pad_lo, pad_hi, n_slots):
    n, d = h1.shape
    tm = TM_ROWDMA
    grid_spec = pltpu.PrefetchScalarGridSpec(
        num_scalar_prefetch=2,
        grid=(n // tm,),
        in_specs=[pl.BlockSpec(memory_space=pl.ANY), pl.BlockSpec((tm, d), lambda i, lo, hi: (i, 0))],
        out_specs=pl.BlockSpec(memory_space=pl.ANY),
        scratch_shapes=[pltpu.SMEM((4 * tm,), jnp.int32), pltpu.VMEM((SUBLANES, d), F32),
                        pltpu.SemaphoreType.DMA((2,)), pltpu.SemaphoreType.DMA(()), pltpu.SemaphoreType.DMA(())],
    )
    return pl.pallas_call(
        _dispatch_body,
        grid_spec=grid_spec,
        out_shape=jax.ShapeDtypeStruct((n_slots, d), F32),
        compiler_params=_cparams(("arbitrary",)),
        name="moe_dispatch",
    )(pad_lo, pad_hi, dest_flat, h1)


def _moe_gemm_body(tile_e_ref, n_active_ref, x_ref, w1_ref, w3_ref, w2_ref, o_ref):
    del tile_e_ref
    i = pl.program_id(0)
    j = pl.program_id(1)

    @pl.when(j == 0)
    def _():
        o_ref[...] = jnp.zeros(o_ref.shape, F32)

    @pl.when(i < n_active_ref[0])
    def _():
        xb = x_ref[...].astype(BF16)
        a = jnp.dot(xb, w1_ref[...], preferred_element_type=F32)
        b = jnp.dot(xb, w3_ref[...], preferred_element_type=F32)
        mid = (a * jax.nn.sigmoid(a) * b).astype(BF16)
        o_ref[...] += jnp.dot(mid, w2_ref[...], preferred_element_type=F32)


def _moe_gemm(xs, w1, w3, w2, first_expert, tile_e, n_active):
    n_slots, d = xs.shape
    f = w1.shape[2]
    tm, fc = TM_MOE, FC_MOE
    grid_spec = pltpu.PrefetchScalarGridSpec(
        num_scalar_prefetch=2,
        grid=(n_slots // tm, f // fc),
        in_specs=[
            pl.BlockSpec((tm, d), lambda i, j, te, na: (i, 0)),
            pl.BlockSpec((None, d, fc), lambda i, j, te, na: (first_expert + te[i], 0, j)),
            pl.BlockSpec((None, d, fc), lambda i, j, te, na: (first_expert + te[i], 0, j)),
            pl.BlockSpec((None, fc, d), lambda i, j, te, na: (first_expert + te[i], j, 0)),
        ],
        out_specs=pl.BlockSpec((tm, d), lambda i, j, te, na: (i, 0)),
    )
    return pl.pallas_call(
        _moe_gemm_body,
        grid_spec=grid_spec,
        out_shape=jax.ShapeDtypeStruct((n_slots, d), F32),
        compiler_params=_cparams(("parallel", "arbitrary")),
        name="moe_gemm",
    )(tile_e, n_active, xs, w1, w3, w2)


def _moe_tail_body(dest_hbm, h_ref, p_ref, g_ref, y_hbm, wg_ref, bg_ref, wp_ref, lng_ref, lnb_ref, o_ref,
                   idx_smem, buf, sem_idx, sem_rows, *, alpha):
    i = pl.program_id(0)
    n_steps = pl.num_programs(0)
    tm = h_ref.shape[0]
    half = i % 2

    def issue_gathers(step):
        hs = step % 2
        ids0 = hs * (2 * tm)

        def issue(t):
            _row_copy(y_hbm, idx_smem[ids0 + 2 * t], buf.at[hs, 0], t, sem_rows.at[hs]).start()
            _row_copy(y_hbm, idx_smem[ids0 + 2 * t + 1], buf.at[hs, 1], t, sem_rows.at[hs]).start()

        _for_token_groups(tm, issue)

    @pl.when(i == 0)
    def _():
        cp = _slot_ids_copy(dest_hbm, idx_smem, sem_idx, i)
        cp.start()
        cp.wait()
        issue_gathers(i)

    @pl.when(i + 1 < n_steps)
    def _():
        cp = _slot_ids_copy(dest_hbm, idx_smem, sem_idx, i + 1)
        cp.start()
        cp.wait()
        issue_gathers(i + 1)

    def drain(t):
        _row_copy(y_hbm, 0, buf.at[half, 0], t, sem_rows.at[half]).wait()
        _row_copy(y_hbm, 0, buf.at[half, 1], t, sem_rows.at[half]).wait()

    _for_token_groups(tm, drain)
    g = g_ref[...]
    f = buf[half, 0] * g[:, 0:1] + buf[half, 1] * g[:, 1:2]
    o_ref[...] = _ple_and_norm(h_ref[...], f, p_ref, wg_ref, bg_ref, wp_ref, lng_ref, lnb_ref, alpha)


def _moe_tail(h1, p_stack, layer, gates_t, y, dest_flat, wg, bg, wp, ln_g, ln_b, alpha):
    n, d = h1.shape
    tm = TM_ROWDMA
    params = [wg, bg, wp, ln_g, ln_b]
    return pl.pallas_call(
        functools.partial(_moe_tail_body, alpha=alpha),
        grid=(n // tm,),
        in_specs=[pl.BlockSpec(memory_space=pl.ANY), _row_spec(tm, d), _layer_rows_spec(tm, p_stack, layer),
                  _row_spec(tm, 2), pl.BlockSpec(memory_space=pl.ANY)] + [_param_spec(a) for a in params],
        out_specs=_row_spec(tm, d),
        out_shape=jax.ShapeDtypeStruct((n, d), F32),
        scratch_shapes=[pltpu.SMEM((4 * tm,), jnp.int32), pltpu.VMEM((2, 2, tm, d), F32),
                        pltpu.SemaphoreType.DMA((2,)), pltpu.SemaphoreType.DMA((2,))],
        compiler_params=_cparams(("arbitrary",)),
        name="moe_tail",
    )(dest_flat, h1, p_stack, gates_t, y, *[_arr(a) for a in params])


def _moe_plan(eidx, n_tokens):
    tm = TM_MOE
    e_flat = eidx.T.reshape(-1)
    onehot = (e_flat[:, None] == jnp.arange(N_EXPERTS, dtype=jnp.int32)[None, :]).astype(jnp.int32)
    csum = jnp.cumsum(onehot, axis=0)
    rank = jnp.sum(csum * onehot, axis=1) - 1
    counts = csum[-1]
    padded = (counts + tm - 1) // tm * tm
    pends = jnp.cumsum(padded)
    pstarts = pends - padded
    dest = (jnp.sum(pstarts[None, :] * onehot, axis=1) + rank).astype(jnp.int32)
    n_tiles = (2 * n_tokens) // tm + N_EXPERTS
    tile_start = jnp.arange(n_tiles, dtype=jnp.int32) * tm
    tile_e = jnp.minimum(jnp.sum((tile_start[:, None] >= pends[None, :]).astype(jnp.int32), axis=1),
                         N_EXPERTS - 1).astype(jnp.int32)
    n_active = (pends[-1] // tm).astype(jnp.int32).reshape(1)
    n_slots = n_tiles * tm
    pad_lo = jnp.concatenate([pstarts + counts, pends[-1:]]).astype(jnp.int32)
    pad_hi = jnp.concatenate([pends, jnp.full((1,), n_slots, pends.dtype)]).astype(jnp.int32)
    return dest, tile_e, n_active, pad_lo, pad_hi, n_slots


def kernel(x, p, ln_in_g, ln_in_b, w_in, b_in, conv_w, conv_b, conv_ln_g, conv_ln_b, rpb, w_out, b_out,
           ln1_g, ln1_b, ffn_w1, ffn_w3, ffn_w2, w_router, moe_w1, moe_w3, moe_w2, ple_w, ple_gate_w,
           ple_gate_b, ln2_g, ln2_b):
    batch, seq_len, d = x.shape
    depth = w_in.shape[0]
    n = batch * seq_len
    alpha = (2.0 * depth) ** 0.25

    rows = lambda v: v.reshape(v.shape[0], 1, v.shape[-1])
    ln_in_g, ln_in_b = ln_in_g.reshape(1, 1, d), ln_in_b.reshape(1, 1, d)
    w_in_b, w_out_b = w_in.astype(BF16), w_out.astype(BF16)
    ffn_w1_b, ffn_w3_b, ffn_w2_b = ffn_w1.astype(BF16), ffn_w3.astype(BF16), ffn_w2.astype(BF16)
    wg_b, wp_b = ple_gate_w.astype(BF16), ple_w.astype(BF16)
    moe_shape = lambda w: w.reshape((-1,) + w.shape[2:])
    moe_w1_b, moe_w3_b, moe_w2_b = (moe_shape(w).astype(BF16) for w in (moe_w1, moe_w3, moe_w2))
    b_in, conv_b, conv_ln_g, conv_ln_b = rows(b_in), rows(conv_b), rows(conv_ln_g), rows(conv_ln_b)
    b_out, ln1_g, ln1_b, ple_gate_b, ln2_g, ln2_b = (rows(v) for v in (b_out, ln1_g, ln1_b, ple_gate_b, ln2_g, ln2_b))
    conv_w_rep = jnp.broadcast_to(conv_w[:, :, None, :], conv_w.shape[:2] + (SUBLANES, conv_w.shape[2]))
    bias_tabs = _attn_bias_tables(rpb)
    p_stack = p.reshape(depth, n, p.shape[-1])

    h = x.reshape(n, d)
    for i in range(depth):
        L = lambda stack, layer=i: _Layered(stack, layer)
        if i == 0:
            h, u, q, k, v = _inproj(h, L(ln_in_g, 0), L(ln_in_b, 0), L(w_in_b), L(b_in), pre_ln=True)
        else:
            u, q, k, v = _inproj(h, L(ln_in_g, 0), L(ln_in_b, 0), L(w_in_b), L(b_in), pre_ln=False)
        conv_out = _conv_module(u, L(conv_w_rep), L(conv_b), L(conv_ln_g), L(conv_ln_b), seq_len)
        attn_out = _attention(q, k, v, bias_tabs, i, batch, seq_len)
        j = i // 2
        tail_params = (L(wg_b), L(ple_gate_b), L(wp_b), L(ln2_g), L(ln2_b))
        mix_params = (L(w_out_b), L(b_out), L(ln1_g), L(ln1_b))
        if i % 2 == 0:
            h = _ffn_tail(h, conv_out, attn_out, *mix_params, p_stack, i, L(ffn_w1_b, j), L(ffn_w3_b, j),
                          L(ffn_w2_b, j), *tail_params, alpha)
        else:
            h1, eidx, gates = _outproj_router(h, conv_out, attn_out, *mix_params, w_router[j].T, alpha)
            dest, tile_e, n_active, pad_lo, pad_hi, n_slots = _moe_plan(eidx, n)
            xs = _dispatch(h1, dest, pad_lo, pad_hi, n_slots)
            y = _moe_gemm(xs, moe_w1_b, moe_w3_b, moe_w2_b, j * N_EXPERTS, tile_e, n_active)
            h = _moe_tail(h1, p_stack, i, gates.T, y, dest, *tail_params, alpha)
    return h.reshape(batch, seq_len, d)
```
